```python
import math
import jax, jax.numpy as jnp
from jax import lax
import numpy as np

D_MODEL = 1024
BATCH = 4
SEQ = 8192
DEPTH = 4

EPS = 1e-6
PLE_DIM = 256
D_FF = 4 * D_MODEL
N_EVEN = (DEPTH + 1) // 2
N_ODD = DEPTH // 2

S5_WIDTH = D_MODEL // 2
S5_GROUP = 16
S5_GROUPS = S5_WIDTH // S5_GROUP
S5_STATE = 64

HG_WIDTH = D_MODEL // 2
HG_HEADS = 4
HG_DK = HG_WIDTH // HG_HEADS
HG_DV = HG_WIDTH // HG_HEADS
HG_CHUNK = 64

EVEN_IN = S5_WIDTH + 4 * HG_WIDTH

N_Q_HEADS = 16
N_KV_HEADS = 4
HEAD_DIM = D_MODEL // N_Q_HEADS
Q_PER_KV = N_Q_HEADS // N_KV_HEADS
WINDOW = 128
ATT_BLOCK = WINDOW
ODD_IN = (N_Q_HEADS + 2 * N_KV_HEADS) * HEAD_DIM

kernel_name = "hybrid_s5_hgrn2_swa_sink_trunk"


def rms_norm(x, g):
    xf = x.astype(jnp.float32)
    y = xf * lax.rsqrt(jnp.mean(xf * xf, axis=-1, keepdims=True) + EPS)
    return (y * g.astype(jnp.float32)).astype(x.dtype)


def s5_mixer(u, lam_re, lam_im, log_dt, b_re, b_im, c_re, c_im, d_skip, w_glu, b_glu):
    f32 = jnp.float32
    bsz, L, _ = u.shape
    uf = u.astype(f32).reshape(bsz, L, S5_GROUPS, S5_GROUP)
    lr = jnp.minimum(lam_re.astype(f32), -1e-4)
    li = lam_im.astype(f32)
    dt = jnp.exp(log_dt.astype(f32))[:, None]
    mag = jnp.exp(lr * dt)
    ar = mag * jnp.cos(li * dt)
    ai = mag * jnp.sin(li * dt)
    den = lr * lr + li * li
    xr = ar - 1.0
    zr = (xr * lr + ai * li) / den
    zi = (ai * lr - xr * li) / den
    br = b_re.astype(f32)
    bi = b_im.astype(f32)
    bbar_re = zr[..., None] * br - zi[..., None] * bi
    bbar_im = zr[..., None] * bi + zi[..., None] * br
    bu_re = jnp.einsum('blgh,gph->blgp', uf, bbar_re)
    bu_im = jnp.einsum('blgh,gph->blgp', uf, bbar_im)
    a_re = jnp.broadcast_to(ar, bu_re.shape)
    a_im = jnp.broadcast_to(ai, bu_im.shape)

    def combine(e1, e2):
        ar1, ai1, br1, bi1 = e1
        ar2, ai2, br2, bi2 = e2
        return (ar2 * ar1 - ai2 * ai1,
                ar2 * ai1 + ai2 * ar1,
                ar2 * br1 - ai2 * bi1 + br2,
                ar2 * bi1 + ai2 * br1 + bi2)

    _, _, s_re, s_im = lax.associative_scan(combine, (a_re, a_im, bu_re, bu_im), axis=1)
    y = (jnp.einsum('blgp,ghp->blgh', s_re, c_re.astype(f32))
         - jnp.einsum('blgp,ghp->blgh', s_im, c_im.astype(f32))
         + d_skip.astype(f32) * uf)
    z = jax.nn.gelu(y.reshape(bsz, L, S5_WIDTH))
    out = z * jax.nn.sigmoid(z @ w_glu.astype(f32) + b_glu.astype(f32))
    return out.astype(u.dtype)


def hgrn2_mixer(q, f_logit, inp, g, lb, head_norm):
    f32 = jnp.float32
    bsz, L, _ = q.shape
    nc = L // HG_CHUNK
    lbf = lb.astype(f32)
    zf = f_logit.astype(f32)
    log_f = jnp.logaddexp(jnp.log(jnp.maximum(lbf, 1e-30)),
                          jnp.log1p(-lbf) + jax.nn.log_sigmoid(zf))
    k = (1.0 - lbf) * jax.nn.sigmoid(-zf)

    def to_chunks(t, d):
        return t.astype(f32).reshape(bsz, nc, HG_CHUNK, HG_HEADS, d).transpose(1, 0, 3, 2, 4)

    qc = to_chunks(q, HG_DK)
    kc = to_chunks(k, HG_DK)
    vc = to_chunks(inp, HG_DV)
    gc = to_chunks(log_f, HG_DK)
    causal = jnp.tril(jnp.ones((HG_CHUNK, HG_CHUNK), dtype=bool))

    def step(S, xs):
        q_c, k_c, v_c, g_c = xs
        b = jnp.cumsum(g_c, axis=2)
        o_inter = jnp.einsum('bhtk,bhkv->bhtv', q_c * jnp.exp(b), S)
        diff = b[:, :, :, None, :] - b[:, :, None, :, :]
        decay = jnp.exp(jnp.where(causal[None, None, :, :, None], diff, -jnp.inf))
        att = jnp.einsum('bhtk,bhtsk,bhsk->bhts', q_c, decay, k_c)
        o = o_inter + jnp.einsum('bhts,bhsv->bhtv', att, v_c)
        b_last = b[:, :, -1:, :]
        S_new = (jnp.exp(b_last[:, :, 0, :])[..., None] * S
                 + jnp.einsum('bhsk,bhsv->bhkv', k_c * jnp.exp(b_last - b), v_c))
        return S_new, o

    S0 = jnp.zeros((bsz, HG_HEADS, HG_DK, HG_DV), f32)
    _, o = lax.scan(step, S0, (qc, kc, vc, gc))
    o = o.transpose(1, 0, 3, 2, 4).reshape(bsz, L, HG_HEADS, HG_DV)
    o = o * lax.rsqrt(jnp.mean(o * o, axis=-1, keepdims=True) + EPS) * head_norm.astype(f32)
    o = o.reshape(bsz, L, HG_WIDTH) * jax.nn.silu(g.astype(f32))
    return o.astype(q.dtype)


def swa_sink_attention(q, k, v, sinks):
    f32 = jnp.float32
    bsz, L = q.shape[0], q.shape[1]
    nb = L // ATT_BLOCK
    qb = q.reshape(bsz, nb, ATT_BLOCK, N_KV_HEADS, Q_PER_KV, HEAD_DIM)

    def window_blocks(t):
        tp = jnp.pad(t, ((0, 0), (ATT_BLOCK, 0), (0, 0), (0, 0)))
        tb = tp.reshape(bsz, nb + 1, ATT_BLOCK, N_KV_HEADS, HEAD_DIM)
        return jnp.concatenate([tb[:, :-1], tb[:, 1:]], axis=2)

    kw = window_blocks(k)
    vw = window_blocks(v)
    scale = 1.0 / math.sqrt(HEAD_DIM)
    scores = jnp.einsum('bntkgd,bnskd->bnkgts', qb, kw).astype(f32) * scale
    t_loc = jnp.arange(ATT_BLOCK)[:, None] + ATT_BLOCK
    s_loc = jnp.arange(2 * ATT_BLOCK)[None, :]
    dist = t_loc - s_loc
    valid = (dist >= 0) & (dist < WINDOW)
    key_pos = jnp.arange(nb)[:, None] * ATT_BLOCK + jnp.arange(2 * ATT_BLOCK)[None, :] - ATT_BLOCK
    valid = valid[None, :, :] & (key_pos >= 0)[:, None, :]
    slopes = jnp.exp2(-8.0 * jnp.arange(1, N_Q_HEADS + 1, dtype=f32) / N_Q_HEADS)
    slopes = slopes.reshape(N_KV_HEADS, Q_PER_KV)
    scores = scores - slopes[:, :, None, None] * dist.astype(f32)
    scores = jnp.where(valid[None, :, None, None, :, :], scores, -jnp.inf)
    sink = sinks.astype(f32).reshape(N_KV_HEADS, Q_PER_KV)[None, None, :, :, None, None]
    m = jnp.maximum(jnp.max(scores, axis=-1, keepdims=True), sink)
    pr = jnp.exp(scores - m)
    pr = pr / (jnp.sum(pr, axis=-1, keepdims=True) + jnp.exp(sink - m))
    out = jnp.einsum('bnkgts,bnskd->bntkgd', pr.astype(v.dtype), vw)
    return out.reshape(bsz, L, N_Q_HEADS * HEAD_DIM)


def setup_inputs(seed: int = 0) -> dict:
    key = jax.random.key(seed)
    ks = jax.random.split(key, 32)
    f32 = jnp.float32

    def nrm(k, shape, scale):
        return jax.random.normal(k, shape, f32) * scale

    def gain(k, shape):
        return 1.0 + 0.02 * jax.random.normal(k, shape, f32)

    lam_im0 = jnp.pi * jnp.arange(S5_STATE, dtype=f32)
    return {
        "x": jax.random.normal(ks[0], (BATCH, SEQ, D_MODEL), f32),
        "p": jax.random.normal(ks[1], (DEPTH, BATCH, SEQ, PLE_DIM), f32),
        "mix_norm": gain(ks[2], (DEPTH, D_MODEL)),
        "mlp_norm": gain(ks[3], (DEPTH, D_MODEL)),
        "ple_norm": gain(ks[4], (DEPTH, D_MODEL)),
        "final_norm": gain(ks[5], (D_MODEL,)),
        "w_in_even": nrm(ks[6], (N_EVEN, D_MODEL, EVEN_IN), D_MODEL ** -0.5),
        "w_out_even": nrm(ks[7], (N_EVEN, S5_WIDTH + HG_WIDTH, D_MODEL), (S5_WIDTH + HG_WIDTH) ** -0.5),
        "s5_lam_re": -0.5 + 0.01 * jax.random.normal(ks[8], (N_EVEN, S5_GROUPS, S5_STATE), f32),
        "s5_lam_im": lam_im0 + 0.01 * jax.random.normal(ks[9], (N_EVEN, S5_GROUPS, S5_STATE), f32),
        "s5_log_dt": jax.random.uniform(ks[10], (N_EVEN, S5_GROUPS), f32,
                                        minval=math.log(1e-3), maxval=math.log(1e-1)),
        "s5_b_re": nrm(ks[11], (N_EVEN, S5_GROUPS, S5_STATE, S5_GROUP), (2 * S5_GROUP) ** -0.5),
        "s5_b_im": nrm(ks[12], (N_EVEN, S5_GROUPS, S5_STATE, S5_GROUP), (2 * S5_GROUP) ** -0.5),
        "s5_c_re": nrm(ks[13], (N_EVEN, S5_GROUPS, S5_GROUP, S5_STATE), S5_STATE ** -0.5),
        "s5_c_im": nrm(ks[14], (N_EVEN, S5_GROUPS, S5_GROUP, S5_STATE), S5_STATE ** -0.5),
        "s5_d": nrm(ks[15], (N_EVEN, S5_GROUPS, S5_GROUP), 1.0),
        "s5_w_glu": nrm(ks[16], (N_EVEN, S5_WIDTH, S5_WIDTH), S5_WIDTH ** -0.5),
        "s5_b_glu": nrm(ks[17], (N_EVEN, S5_WIDTH), 0.01),
        "hgrn_lb_logits": nrm(ks[18], (N_EVEN, HG_WIDTH), 1.0),
        "hgrn_norm": gain(ks[19], (N_EVEN, HG_HEADS, HG_DV)),
        "w_qkv_odd": nrm(ks[20], (N_ODD, D_MODEL, ODD_IN), D_MODEL ** -0.5),
        "w_o_odd": nrm(ks[21], (N_ODD, N_Q_HEADS * HEAD_DIM, D_MODEL), (N_Q_HEADS * HEAD_DIM) ** -0.5),
        "attn_sinks": nrm(ks[22], (N_ODD, N_Q_HEADS), 0.5),
        "w_mlp_in": nrm(ks[23], (DEPTH, D_MODEL, D_FF), D_MODEL ** -0.5),
        "w_mlp_out": nrm(ks[24], (DEPTH, D_FF, D_MODEL), D_FF ** -0.5),
        "w_ple_up": nrm(ks[25], (DEPTH, PLE_DIM, D_MODEL), PLE_DIM ** -0.5),
        "w_ple_gate": nrm(ks[26], (DEPTH, D_MODEL, D_MODEL), D_MODEL ** -0.5),
    }


def reference(x, p, mix_norm, mlp_norm, ple_norm, final_norm, w_in_even, w_out_even,
              s5_lam_re, s5_lam_im, s5_log_dt, s5_b_re, s5_b_im, s5_c_re, s5_c_im, s5_d,
              s5_w_glu, s5_b_glu, hgrn_lb_logits, hgrn_norm, w_qkv_odd, w_o_odd, attn_sinks,
              w_mlp_in, w_mlp_out, w_ple_up, w_ple_gate):
    bsz, L, _ = x.shape
    lb_sm = jax.nn.softmax(hgrn_lb_logits.astype(jnp.float32), axis=0)
    lower_bounds = jnp.cumsum(lb_sm, axis=0) - lb_sm[0:1]
    h = x
    for i in range(DEPTH):
        j = i // 2
        hn = rms_norm(h, mix_norm[i])
        if i % 2 == 0:
            proj = hn @ w_in_even[j]
            u, q_b, f_b, i_b, g_b = jnp.split(
                proj, [S5_WIDTH, S5_WIDTH + HG_WIDTH, S5_WIDTH + 2 * HG_WIDTH,
                       S5_WIDTH + 3 * HG_WIDTH], axis=-1)
            y_a = s5_mixer(u, s5_lam_re[j], s5_lam_im[j], s5_log_dt[j], s5_b_re[j], s5_b_im[j],
                           s5_c_re[j], s5_c_im[j], s5_d[j], s5_w_glu[j], s5_b_glu[j])
            y_b = hgrn2_mixer(q_b, f_b, i_b, g_b, lower_bounds[j], hgrn_norm[j])
            h = h + jnp.concatenate([y_a, y_b], axis=-1) @ w_out_even[j]
        else:
            proj = hn @ w_qkv_odd[j]
            nq = N_Q_HEADS * HEAD_DIM
            nkv = N_KV_HEADS * HEAD_DIM
            q = proj[..., :nq].reshape(bsz, L, N_Q_HEADS, HEAD_DIM)
            k = proj[..., nq:nq + nkv].reshape(bsz, L, N_KV_HEADS, HEAD_DIM)
            v = proj[..., nq + nkv:].reshape(bsz, L, N_KV_HEADS, HEAD_DIM)
            h = h + swa_sink_attention(q, k, v, attn_sinks[j]) @ w_o_odd[j]
        hn = rms_norm(h, mlp_norm[i])
        h = h + jnp.square(jax.nn.relu(hn @ w_mlp_in[i])) @ w_mlp_out[i]
        hn = rms_norm(h, ple_norm[i])
        h = h + (p[i] @ w_ple_up[i]) * jax.nn.sigmoid(hn @ w_ple_gate[i])
    return rms_norm(h, final_norm)
```

```python
import functools
import math

import jax
import jax.numpy as jnp
from jax import lax
from jax.experimental import pallas as pl
from jax.experimental.pallas import tpu as pltpu

F32 = jnp.float32
BF16 = jnp.bfloat16

EPS = 1e-6
WINDOW = 128
S5_CHUNK = 64
HG_CHUNK = 128
LAM_RE_MAX = -1e-4
NEG_BIG = -1e30

_V7X_VMEM_BYTES = 64 * 1024 * 1024
_VMEM_LIMIT = _V7X_VMEM_BYTES - 12 * 1024 * 1024

TOKEN_BLOCK = 512
ATTN_BLOCK_Q = 512


def _cparams(*sem):
    return pltpu.CompilerParams(dimension_semantics=sem, vmem_limit_bytes=_VMEM_LIMIT)


def _const_spec(shape):
    nd = len(shape)
    return pl.BlockSpec(shape, lambda *_: (0,) * nd, pipeline_mode=pl.Buffered(1))


def _rms(x, g):
    ms = jnp.mean(x * x, axis=-1, keepdims=True)
    return x * lax.rsqrt(ms + EPS) * g


def _sigmoid(x):
    return 1.0 / (1.0 + jnp.exp(-x))


def _bdot(a, b):
    return jnp.dot(a.astype(BF16), b.astype(BF16), preferred_element_type=F32)


def _split3(x):
    hi = x.astype(BF16)
    r = x - hi.astype(F32)
    mid = r.astype(BF16)
    lo = (r - mid.astype(F32)).astype(BF16)
    return hi, mid, lo


def _proj_kernel(h_ref, g_ref, w_ref, *out_refs, scales):
    hn = _rms(h_ref[...], g_ref[...]).astype(BF16)
    off = 0
    for o_ref, s in zip(out_refs, scales):
        n = o_ref.shape[-1]
        acc = jnp.dot(hn, w_ref[:, off:off + n], preferred_element_type=F32)
        if s != 1.0:
            acc = acc * s
        o_ref[...] = acc.astype(o_ref.dtype)
        off += n


def _norm_proj(h, g, w, outs):
    m, d = h.shape
    tm = TOKEN_BLOCK
    kern = functools.partial(_proj_kernel, scales=tuple(s for _, _, s in outs))
    return pl.pallas_call(
        kern,
        grid=(m // tm,),
        in_specs=[pl.BlockSpec((tm, d), lambda i: (i, 0)),
                  _const_spec((1, d)),
                  _const_spec(w.shape)],
        out_specs=[pl.BlockSpec((tm, n), lambda i: (i, 0)) for n, _, _ in outs],
        out_shape=[jax.ShapeDtypeStruct((m, n), dt) for n, dt, _ in outs],
        compiler_params=_cparams("parallel"),
        name="norm_proj",
    )(h, g.reshape(1, d), w)


def _cmul(ar, ai, br, bi):
    return ar * br - ai * bi, ar * bi + ai * br


def _s5_discretize(lam_re, lam_im, dt):
    lr = jnp.minimum(lam_re, LAM_RE_MAX)
    li = lam_im
    mag = jnp.exp(lr * dt)
    ar = mag * jnp.cos(li * dt)
    ai = mag * jnp.sin(li * dt)
    den = lr * lr + li * li
    xr = ar - 1.0
    zr = (xr * lr + ai * li) / den
    zi = (ai * lr - xr * li) / den
    return ar, ai, zr, zi


def _squarings(ar, ai, n):
    out = [(ar, ai)]
    for _ in range(n - 1):
        ar, ai = _cmul(ar, ai, ar, ai)
        out.append((ar, ai))
    return out


def _power_table(sq, tau):
    pr = jnp.ones(tau.shape, F32)
    pi = jnp.zeros(tau.shape, F32)
    for j, (sr, si) in enumerate(sq):
        bit = ((tau >> j) & 1) == 1
        nr, ni = _cmul(pr, pi, sr, si)
        pr = jnp.where(bit, nr, pr)
        pi = jnp.where(bit, ni, pi)
    return pr, pi


def _s5_kernel(u_ref, lamc_re_ref, lamc_im_ref, lamr_re_ref, lamr_im_ref, logdt_ref,
               b_re_ref, b_im_ref, ct_re_ref, ct_im_ref, bt_re_ref, bt_im_ref, d_ref,
               y_ref, m_ref, *, n_batch, chunk, n_state, n_chan):
    t, p, hh = chunk, n_state, n_chan
    tw = t * hh
    n_tbits = t.bit_length()
    rows = u_ref.shape[1]
    nc = rows // n_batch
    n_cbits = max(1, (nc - 1).bit_length())

    dt = jnp.exp(logdt_ref[0])

    ar, ai, zr, zi = _s5_discretize(lamc_re_ref[0], lamc_im_ref[0], dt)
    sq = _squarings(ar, ai, n_tbits)
    j = lax.broadcasted_iota(jnp.int32, (p, 2 * tw), 1) // hh
    j1 = lax.broadcasted_iota(jnp.int32, (p, tw), 1) // hh

    bbr, bbi = _cmul(zr, zi, b_re_ref[0], b_im_ref[0])
    pr, pi = _power_table(sq, (t - 1) - j1)
    qr, qi = _cmul(pr, pi, bbr, bbi)
    q_t = jnp.concatenate([qr, qi], axis=0).astype(BF16)

    ctr, cti = ct_re_ref[0], ct_im_ref[0]
    pr, pi = _power_table(sq, j1 + 1)
    car, cai = _cmul(ct_re_ref[0, :, 0:tw], ct_im_ref[0, :, 0:tw], pr, pi)
    p_m = jnp.concatenate([car, -cai], axis=0).astype(BF16)

    pr, pi = _power_table(sq, jnp.maximum(j - t, 0))
    car, cai = _cmul(ctr, cti, pr, pi)
    live = j >= t
    rhs = jnp.concatenate([jnp.where(live, car, 0.0), jnp.where(live, cai, 0.0)], axis=0)

    ar2, ai2, zr2, zi2 = _s5_discretize(lamr_re_ref[0], lamr_im_ref[0], dt)
    lane2 = lax.broadcasted_iota(jnp.int32, (1, 2 * p), 1)
    first = lane2 < p
    btr, bti = bt_re_ref[0], bt_im_ref[0]
    lhs = jnp.where(first, zr2 * btr - zi2 * bti, -(zr2 * bti + zi2 * btr))

    lh, ll = lhs.astype(BF16), (lhs - lhs.astype(BF16).astype(F32)).astype(BF16)
    rh, rl = rhs.astype(BF16), (rhs - rhs.astype(BF16).astype(F32)).astype(BF16)
    z = (jnp.dot(lh, rh, preferred_element_type=F32)
         + jnp.dot(lh, rl, preferred_element_type=F32)
         + jnp.dot(ll, rh, preferred_element_type=F32))
    lane_z = lax.broadcasted_iota(jnp.int32, (hh, 2 * tw), 1)
    row_z = lax.broadcasted_iota(jnp.int32, (hh, 2 * tw), 0)
    z = z + jnp.where((lane_z // hh == t) & (lane_z % hh == row_z), d_ref[0], 0.0)
    for s in range(t):
        off = (t - s) * hh
        m_ref[s * hh:(s + 1) * hh, :] = pltpu.roll(z, 2 * tw - off, axis=1)[:, :tw].astype(BF16)

    u = u_ref[0]
    y = jnp.dot(u, m_ref[...], preferred_element_type=F32)
    x = lax.dot_general(u, q_t, (((1,), (1,)), ((), ())), preferred_element_type=F32)

    sqc = _squarings(ar2, ai2, n_tbits + n_cbits - 1)[n_tbits - 1:]
    sign = jnp.where(first, -1.0, 1.0)
    ridx = lax.broadcasted_iota(jnp.int32, (nc, 2 * p), 0)

    def shift_down(v, k):
        if k % 8 == 0:
            return jnp.concatenate([jnp.zeros((k, 2 * p), F32), v[:nc - k]], axis=0)
        return jnp.where(ridx >= k, pltpu.roll(v, k, axis=0), 0.0)

    parts = []
    for b in range(n_batch):
        sb = x[b * nc:(b + 1) * nc]
        for k in range(n_cbits):
            pr_k, pi_k = sqc[k]
            sh = shift_down(sb, 1 << k)
            sb = sb + pr_k * sh + (sign * pi_k) * pltpu.roll(sh, p, axis=1)
        parts.append(shift_down(sb, 1))
    s_prev = jnp.concatenate(parts, axis=0)
    y = y + jnp.dot(s_prev.astype(BF16), p_m, preferred_element_type=F32)
    y_ref[0] = y


def _s5_mix(u, lam_re, lam_im, log_dt, b_re, b_im, c_re, c_im, d_skip, n_batch):
    m, width = u.shape
    g, p, hh = b_re.shape
    t = S5_CHUNK
    tw = t * hh
    seq = m // n_batch
    nc = seq // t
    rows = n_batch * nc
    ug = u.reshape(n_batch, nc, t, g, hh).transpose(3, 0, 1, 2, 4).reshape(g, rows, tw)

    col = lambda a: a.reshape(g, p, 1)
    row2 = lambda a: jnp.concatenate([a, a], axis=-1).reshape(g, 1, 2 * p)
    b_re_t = jnp.tile(b_re, (1, 1, t))
    b_im_t = jnp.tile(b_im, (1, 1, t))
    ct_re_t = jnp.tile(c_re.transpose(0, 2, 1), (1, 1, 2 * t))
    ct_im_t = jnp.tile(c_im.transpose(0, 2, 1), (1, 1, 2 * t))
    bt_re2 = jnp.tile(b_re.transpose(0, 2, 1), (1, 1, 2))
    bt_im2 = jnp.tile(b_im.transpose(0, 2, 1), (1, 1, 2))
    d_t = jnp.tile(d_skip, (1, 2 * t)).reshape(g, 1, 2 * tw)

    gspec = lambda *shape: pl.BlockSpec((1,) + shape, lambda i: (i,) + (0,) * len(shape))
    kern = functools.partial(_s5_kernel, n_batch=n_batch, chunk=t, n_state=p, n_chan=hh)
    yg = pl.pallas_call(
        kern,
        grid=(g,),
        in_specs=[gspec(rows, tw),
                  gspec(p, 1), gspec(p, 1), gspec(1, 2 * p), gspec(1, 2 * p), gspec(1, 1),
                  gspec(p, tw), gspec(p, tw), gspec(p, 2 * tw), gspec(p, 2 * tw),
                  gspec(hh, 2 * p), gspec(hh, 2 * p), gspec(1, 2 * tw)],
        out_specs=gspec(rows, tw),
        out_shape=jax.ShapeDtypeStruct((g, rows, tw), F32),
        scratch_shapes=[pltpu.VMEM((tw, tw), BF16)],
        compiler_params=_cparams("parallel"),
        name="s5_mix",
    )(ug, col(lam_re), col(lam_im), row2(lam_re), row2(lam_im), log_dt.reshape(g, 1, 1),
      b_re_t, b_im_t, ct_re_t, ct_im_t, bt_re2, bt_im2, d_t)
    return yg.reshape(g, n_batch, nc, t, hh).transpose(1, 2, 3, 0, 4).reshape(m, width)


def _hgrn_kernel(x_ref, lbl_ref, hn_ref, o_ref, st_ref, *, layer, n_heads):
    c = x_ref.shape[0]
    width = o_ref.shape[-1]
    dk = width // n_heads
    n_lev = c.bit_length() - 1

    @pl.when(pl.program_id(1) == 0)
    def _():
        st_ref[...] = jnp.zeros_like(st_ref)

    lg = lbl_ref[...]
    e = jnp.exp(lg - jnp.max(lg, axis=0, keepdims=True))
    sm = e / jnp.sum(e, axis=0, keepdims=True)
    lb = jnp.zeros((1, width), F32)
    for r in range(1, layer + 1):
        lb = lb + sm[r:r + 1]

    q = x_ref[:, 0:width]
    zf = x_ref[:, width:2 * width]
    v = x_ref[:, 2 * width:3 * width]
    gate = x_ref[:, 3 * width:4 * width]

    lsig = jnp.minimum(zf, 0.0) - jnp.log1p(jnp.exp(-jnp.abs(zf)))
    a1 = jnp.log(jnp.maximum(lb, 1e-30))
    a2 = jnp.log1p(-lb) + lsig
    logf = jnp.maximum(a1, a2) + jnp.log1p(jnp.exp(-jnp.abs(a1 - a2)))
    k = (1.0 - lb) * _sigmoid(-zf)

    ri = lax.broadcasted_iota(jnp.int32, (c, c), 0)
    ci = lax.broadcasted_iota(jnp.int32, (c, c), 1)
    tri = jnp.where(ci <= ri, 1.0, 0.0).astype(BF16)
    hi, mid, lo = _split3(logf)
    b = (jnp.dot(tri, hi, preferred_element_type=F32)
         + jnp.dot(tri, mid, preferred_element_type=F32)
         + jnp.dot(tri, lo, preferred_element_type=F32))

    row = lax.broadcasted_iota(jnp.int32, (c, width), 0)
    qk = []
    keeps = []
    for l in range(1, n_lev + 1):
        blk, half = 1 << l, 1 << (l - 1)
        if half >= 8:
            pieces = []
            for s0 in range(0, c, blk):
                bd = b[s0 + half - 1:s0 + half]
                pieces.append(bd - b[s0:s0 + half])
                pieces.append(b[s0 + half:s0 + blk] - bd)
            ex = jnp.concatenate(pieces, axis=0)
        else:
            off = (row % blk) - (half - 1)
            bd = b
            for sh in range(1, half + 1):
                bd = jnp.where(off == sh, pltpu.roll(b, sh, axis=0), bd)
                if sh < half:
                    bd = jnp.where(off == -sh, pltpu.roll(b, c - sh, axis=0), bd)
            ex = jnp.where(off > 0, b - bd, bd - b)
        w = jnp.exp(ex)
        qk.append(((q * w).astype(BF16), (k * w).astype(BF16)))
        keeps.append((ri // blk == ci // blk) & (ri % blk >= half) & (ci % blk < half))

    eb = jnp.exp(b)
    b_last = b[c - 1:c]
    q_in = (q * eb).astype(BF16)
    k_out = (k * jnp.exp(b_last - b)).astype(BF16)
    e_last = jnp.exp(b_last)
    vb = v.astype(BF16)
    qkd = q * k

    nt = (((1,), (1,)), ((), ()))
    tn = (((0,), (0,)), ((), ()))
    hnw = hn_ref[...]
    for h in range(n_heads):
        sl = slice(h * dk, (h + 1) * dk)
        att = jnp.where(ri == ci, jnp.sum(qkd[:, sl], axis=-1, keepdims=True), 0.0)
        for l in range(1, n_lev + 1):
            ql, kl = qk[l - 1]
            pm = lax.dot_general(ql[:, sl], kl[:, sl], nt, preferred_element_type=F32)
            att = att + jnp.where(keeps[l - 1], pm, 0.0)
        st = st_ref[h]
        o = lax.dot_general(q_in[:, sl], st.astype(BF16), nt, preferred_element_type=F32)
        o = o + jnp.dot(att.astype(BF16), vb[:, sl], preferred_element_type=F32)
        st_ref[h] = st * e_last[:, sl] + lax.dot_general(vb[:, sl], k_out[:, sl], tn,
                                                          preferred_element_type=F32)
        o = o * lax.rsqrt(jnp.mean(o * o, axis=-1, keepdims=True) + EPS) * hnw[:, sl]
        gh = gate[:, sl]
        o_ref[:, sl] = o * (gh * _sigmoid(gh))


def _hgrn_mix(x, lb_logits, head_norm, layer, n_batch):
    m = x.shape[0]
    n_layers, width = lb_logits.shape
    n_heads = head_norm.shape[0]
    dk = width // n_heads
    c = HG_CHUNK
    per = m // n_batch // c
    kern = functools.partial(_hgrn_kernel, layer=layer, n_heads=n_heads)
    return pl.pallas_call(
        kern,
        grid=(n_batch, per),
        in_specs=[pl.BlockSpec((c, 4 * width), lambda b, i: (b * per + i, 0)),
                  _const_spec((n_layers, width)),
                  _const_spec((1, width))],
        out_specs=pl.BlockSpec((c, width), lambda b, i: (b * per + i, 0)),
        out_shape=jax.ShapeDtypeStruct((m, width), F32),
        scratch_shapes=[pltpu.VMEM((n_heads, dk, dk), F32)],
        compiler_params=_cparams("parallel", "arbitrary"),
        name="hgrn_mix",
    )(x, lb_logits, head_norm.reshape(1, width))


def _attn_kernel(sink_ref, q_ref, kv_ref, kvp_ref, o_ref, *, n_q, n_kv):
    tq = q_ref.shape[0]
    hd = q_ref.shape[1] // n_q
    grp = n_q // n_kv
    w = WINDOW
    first_block = pl.program_id(1) == 0

    rr = lax.broadcasted_iota(jnp.int32, (grp * w, 2 * w), 0)
    cc = lax.broadcasted_iota(jnp.int32, (grp * w, 2 * w), 1)
    tpos = rr % w
    dist = tpos + w - cc
    valid = (dist >= 0) & (dist < w)
    valid0 = valid & (cc >= jnp.where(first_block, w, 0))
    distf = dist.astype(F32)
    gidx = rr[:, 0:1] // w

    nt = (((1,), (1,)), ((), ()))
    for jb in range(tq // w):
        rs = slice(jb * w, (jb + 1) * w)
        mask = valid0 if jb == 0 else valid
        for kh in range(n_kv):
            ks = slice(kh * hd, (kh + 1) * hd)
            vs = slice((n_kv + kh) * hd, (n_kv + kh + 1) * hd)
            if jb == 0:
                kprev, vprev = kvp_ref[:, ks], kvp_ref[:, vs]
            else:
                ps = slice((jb - 1) * w, jb * w)
                kprev, vprev = kv_ref[ps, ks], kv_ref[ps, vs]
            kw = jnp.concatenate([kprev, kv_ref[rs, ks]], axis=0)
            vw = jnp.concatenate([vprev, kv_ref[rs, vs]], axis=0)
            qs = jnp.concatenate(
                [q_ref[rs, (kh * grp + g) * hd:(kh * grp + g + 1) * hd] for g in range(grp)], axis=0)
            s = lax.dot_general(qs, kw, nt, preferred_element_type=F32)
            slope = jnp.zeros((grp * w, 1), F32)
            sink = jnp.zeros((grp * w, 1), F32)
            for g in range(grp):
                head = kh * grp + g
                slope = jnp.where(gidx == g, 2.0 ** (-8.0 * (head + 1) / n_q), slope)
                sink = jnp.where(gidx == g, sink_ref[head], sink)
            s = jnp.where(mask, s - slope * distf, NEG_BIG)
            mx = jnp.maximum(jnp.max(s, axis=-1, keepdims=True), sink)
            pe = jnp.exp(s - mx)
            den = jnp.sum(pe, axis=-1, keepdims=True) + jnp.exp(sink - mx)
            o = jnp.dot(pe.astype(BF16), vw, preferred_element_type=F32) / den
            for g in range(grp):
                head = kh * grp + g
                o_ref[rs, head * hd:(head + 1) * hd] = o[g * w:(g + 1) * w].astype(o_ref.dtype)


def _attention(q, kv, sinks, n_batch):
    m, qw = q.shape
    n_q = sinks.shape[0]
    hd = qw // n_q
    n_kv = kv.shape[1] // (2 * hd)
    tq = ATTN_BLOCK_Q
    per = m // n_batch // tq
    ratio = tq // WINDOW
    kern = functools.partial(_attn_kernel, n_q=n_q, n_kv=n_kv)
    return pl.pallas_call(
        kern,
        grid=(n_batch, per),
        in_specs=[pl.BlockSpec(memory_space=pltpu.SMEM),
                  pl.BlockSpec((tq, qw), lambda b, i: (b * per + i, 0)),
                  pl.BlockSpec((tq, kv.shape[1]), lambda b, i: (b * per + i, 0)),
                  pl.BlockSpec((WINDOW, kv.shape[1]),
                               lambda b, i: (jnp.maximum((b * per + i) * ratio - 1, 0), 0))],
        out_specs=pl.BlockSpec((tq, qw), lambda b, i: (b * per + i, 0)),
        out_shape=jax.ShapeDtypeStruct((m, qw), BF16),
        compiler_params=_cparams("parallel", "parallel"),
        name="swa_attention",
    )(sinks, q, kv, kv)


def _tail_kernel(*refs, even, final, ff_block):
    refs = list(refs)
    o_ref = refs.pop()
    h_ref = refs.pop(0)
    if even:
        ya_ref, yb_ref, wglu_ref, bglu_ref = refs[:4]
        refs = refs[4:]
    else:
        a_ref = refs.pop(0)
    p_ref, wout_ref, g2_ref, w1_ref, w2_ref, g3_ref, wup_ref, wgate_ref = refs[:8]
    gf_ref = refs[8] if final else None

    h = h_ref[...]
    if even:
        z = jax.nn.gelu(ya_ref[...])
        oa = z * _sigmoid(_bdot(z, wglu_ref[...]) + bglu_ref[...])
        wa = oa.shape[-1]
        mix = (_bdot(oa, wout_ref[0:wa, :]) + _bdot(yb_ref[...], wout_ref[wa:, :]))
    else:
        mix = _bdot(a_ref[...], wout_ref[...])
    h = h + mix

    hn = _rms(h, g2_ref[...]).astype(BF16)
    d_ff = w1_ref.shape[1]
    acc = jnp.zeros_like(h)
    for c0 in range(0, d_ff, ff_block):
        a = jnp.dot(hn, w1_ref[:, c0:c0 + ff_block], preferred_element_type=F32)
        a = jnp.square(jnp.maximum(a, 0.0)).astype(BF16)
        acc = acc + jnp.dot(a, w2_ref[c0:c0 + ff_block, :], preferred_element_type=F32)
    h = h + acc

    hn = _rms(h, g3_ref[...])
    gate = _sigmoid(_bdot(hn, wgate_ref[...]))
    h = h + _bdot(p_ref[...], wup_ref[...]) * gate
    if final:
        h = _rms(h, gf_ref[...])
    o_ref[...] = h


def _tail(h, mixer_ins, mixer_weights, p, w_out, g2, w1, w2, g3, w_up, w_gate, g_final, even):
    m, d = h.shape
    tm = TOKEN_BLOCK
    tok = lambda a: pl.BlockSpec((tm, a.shape[1]), lambda i: (i, 0))
    rowv = lambda a: a.reshape(1, -1)
    weights = [w_out, rowv(g2), w1, w2, rowv(g3), w_up, w_gate]
    final = g_final is not None
    if final:
        weights.append(rowv(g_final))
    args = [h, *mixer_ins, *mixer_weights, p, *weights]
    specs = ([tok(h)] + [tok(a) for a in mixer_ins] + [_const_spec(a.shape) for a in mixer_weights]
             + [tok(p)] + [_const_spec(a.shape) for a in weights])
    kern = functools.partial(_tail_kernel, even=even, final=final, ff_block=1024)
    return pl.pallas_call(
        kern,
        grid=(m // tm,),
        in_specs=specs,
        out_specs=pl.BlockSpec((tm, d), lambda i: (i, 0)),
        out_shape=jax.ShapeDtypeStruct((m, d), F32),
        compiler_params=_cparams("parallel"),
        name="layer_tail",
    )(*args)


def kernel(x, p, mix_norm, mlp_norm, ple_norm, final_norm, w_in_even, w_out_even, s5_lam_re, s5_lam_im, s5_log_dt, s5_b_re, s5_b_im, s5_c_re, s5_c_im, s5_d, s5_w_glu, s5_b_glu, hgrn_lb_logits, hgrn_norm, w_qkv_odd, w_o_odd, attn_sinks, w_mlp_in, w_mlp_out, w_ple_up, w_ple_gate):
    bsz, seq, d = x.shape
    depth = p.shape[0]
    m = bsz * seq
    s5_w = s5_w_glu.shape[-1]
    hg_w = hgrn_lb_logits.shape[-1]
    n_q = attn_sinks.shape[-1]
    hd = d // n_q
    kv_w = w_qkv_odd.shape[-1] - n_q * hd
    assert seq % max(TOKEN_BLOCK, ATTN_BLOCK_Q, HG_CHUNK, S5_CHUNK) == 0

    h = x.reshape(m, d)
    for i in range(depth):
        j = i // 2
        if i % 2 == 0:
            u, hg = _norm_proj(h, mix_norm[i], w_in_even[j].astype(BF16),
                               [(s5_w, BF16, 1.0), (4 * hg_w, F32, 1.0)])
            ya = _s5_mix(u, s5_lam_re[j], s5_lam_im[j], s5_log_dt[j], s5_b_re[j], s5_b_im[j],
                         s5_c_re[j], s5_c_im[j], s5_d[j], bsz)
            yb = _hgrn_mix(hg, hgrn_lb_logits, hgrn_norm[j], j, bsz)
            mixer_ins = [ya, yb]
            mixer_weights = [s5_w_glu[j].astype(BF16), s5_b_glu[j].reshape(1, -1)]
            w_out = w_out_even[j]
        else:
            q, kv = _norm_proj(h, mix_norm[i], w_qkv_odd[j].astype(BF16),
                               [(n_q * hd, BF16, 1.0 / math.sqrt(hd)), (kv_w, BF16, 1.0)])
            mixer_ins = [_attention(q, kv, attn_sinks[j], bsz)]
            mixer_weights = []
            w_out = w_o_odd[j]
        h = _tail(h, mixer_ins, mixer_weights, p[i].reshape(m, -1), w_out.astype(BF16),
                  mlp_norm[i], w_mlp_in[i].astype(BF16), w_mlp_out[i].astype(BF16),
                  ple_norm[i], w_ple_up[i].astype(BF16), w_ple_gate[i].astype(BF16),
                  final_norm if i == depth - 1 else None, even=(i % 2 == 0))
    return h.reshape(bsz, seq, d)
```

```python
import functools
import math

import jax
import jax.numpy as jnp
from jax import lax
from jax.experimental import pallas as pl
from jax.experimental.pallas import tpu as pltpu

F32 = jnp.float32
BF16 = jnp.bfloat16

EPS = 1e-6
WINDOW = 128
S5_CHUNK = 64
HG_CHUNK = 128
HG_BLOCK = 512
LAM_RE_MAX = -1e-4
NEG_BIG = -1e30
_LOG2E = math.log2(math.e)

_V7X_VMEM_BYTES = 64 * 1024 * 1024
_VMEM_LIMIT = _V7X_VMEM_BYTES - 12 * 1024 * 1024

TOKEN_BLOCK = 512
ATTN_BLOCK_Q = 512
ATTN_AHEAD = 2


def _cparams(*sem):
    return pltpu.CompilerParams(dimension_semantics=sem, vmem_limit_bytes=_VMEM_LIMIT)


def _const_spec(shape):
    nd = len(shape)
    return pl.BlockSpec(shape, lambda *_: (0,) * nd, pipeline_mode=pl.Buffered(1))


def _rms(x, g):
    ms = jnp.mean(x * x, axis=-1, keepdims=True)
    return x * lax.rsqrt(ms + EPS) * g


def _sigmoid(x):
    return 1.0 / (1.0 + jnp.exp(-x))


def _bdot(a, b):
    return jnp.dot(a.astype(BF16), b.astype(BF16), preferred_element_type=F32)


def _split3(x):
    hi = x.astype(BF16)
    r = x - hi.astype(F32)
    mid = r.astype(BF16)
    lo = (r - mid.astype(F32)).astype(BF16)
    return hi, mid, lo


def _proj_kernel(h_ref, g_ref, w_ref, *out_refs, scales):
    hn = _rms(h_ref[...], g_ref[...]).astype(BF16)
    off = 0
    for o_ref, s in zip(out_refs, scales):
        n = o_ref.shape[-1]
        acc = jnp.dot(hn, w_ref[:, off:off + n], preferred_element_type=F32)
        if s != 1.0:
            acc = acc * s
        o_ref[...] = acc.astype(o_ref.dtype)
        off += n


def _norm_proj(h, g, w, outs):
    m, d = h.shape
    tm = TOKEN_BLOCK
    kern = functools.partial(_proj_kernel, scales=tuple(s for _, _, s in outs))
    return pl.pallas_call(
        kern,
        grid=(m // tm,),
        in_specs=[pl.BlockSpec((tm, d), lambda i: (i, 0)),
                  _const_spec((1, d)),
                  _const_spec(w.shape)],
        out_specs=[pl.BlockSpec((tm, n), lambda i: (i, 0)) for n, _, _ in outs],
        out_shape=[jax.ShapeDtypeStruct((m, n), dt) for n, dt, _ in outs],
        compiler_params=_cparams("parallel"),
        name="norm_proj",
    )(h, g.reshape(1, d), w)


def _cmul(ar, ai, br, bi):
    return ar * br - ai * bi, ar * bi + ai * br


def _s5_discretize(lam_re, lam_im, dt):
    lr = jnp.minimum(lam_re, LAM_RE_MAX)
    li = lam_im
    mag = jnp.exp(lr * dt)
    ar = mag * jnp.cos(li * dt)
    ai = mag * jnp.sin(li * dt)
    den = lr * lr + li * li
    xr = ar - 1.0
    zr = (xr * lr + ai * li) / den
    zi = (ai * lr - xr * li) / den
    return ar, ai, zr, zi


def _squarings(ar, ai, n):
    out = [(ar, ai)]
    for _ in range(n - 1):
        ar, ai = _cmul(ar, ai, ar, ai)
        out.append((ar, ai))
    return out


def _power_table(sq, tau):
    pr = jnp.ones(tau.shape, F32)
    pi = jnp.zeros(tau.shape, F32)
    for j, (sr, si) in enumerate(sq):
        bit = ((tau >> j) & 1) == 1
        nr, ni = _cmul(pr, pi, sr, si)
        pr = jnp.where(bit, nr, pr)
        pi = jnp.where(bit, ni, pi)
    return pr, pi


def _s5_kernel(u_ref, lamc_re_ref, lamc_im_ref, lamr_re_ref, lamr_im_ref, logdt_ref,
               b_re_ref, b_im_ref, ct_re_ref, ct_im_ref, bt_re_ref, bt_im_ref, d_ref,
               y_ref, m_ref, *, n_batch, chunk, n_state, n_chan):
    t, p, hh = chunk, n_state, n_chan
    tw = t * hh
    n_tbits = t.bit_length()
    rows = u_ref.shape[1]
    nc = rows // n_batch
    n_cbits = max(1, (nc - 1).bit_length())

    dt = jnp.exp(logdt_ref[0])

    ar, ai, zr, zi = _s5_discretize(lamc_re_ref[0], lamc_im_ref[0], dt)
    sq = _squarings(ar, ai, n_tbits)
    j = lax.broadcasted_iota(jnp.int32, (p, 2 * tw), 1) // hh
    j1 = lax.broadcasted_iota(jnp.int32, (p, tw), 1) // hh

    bbr, bbi = _cmul(zr, zi, b_re_ref[0], b_im_ref[0])
    pr, pi = _power_table(sq, (t - 1) - j1)
    qr, qi = _cmul(pr, pi, bbr, bbi)
    q_t = jnp.concatenate([qr, qi], axis=0).astype(BF16)

    ctr, cti = ct_re_ref[0], ct_im_ref[0]
    pr, pi = _power_table(sq, j1 + 1)
    car, cai = _cmul(ct_re_ref[0, :, 0:tw], ct_im_ref[0, :, 0:tw], pr, pi)
    p_m = jnp.concatenate([car, -cai], axis=0).astype(BF16)

    pr, pi = _power_table(sq, jnp.maximum(j - t, 0))
    car, cai = _cmul(ctr, cti, pr, pi)
    live = j >= t
    rhs = jnp.concatenate([jnp.where(live, car, 0.0), jnp.where(live, cai, 0.0)], axis=0)

    ar2, ai2, zr2, zi2 = _s5_discretize(lamr_re_ref[0], lamr_im_ref[0], dt)
    lane2 = lax.broadcasted_iota(jnp.int32, (1, 2 * p), 1)
    first = lane2 < p
    btr, bti = bt_re_ref[0], bt_im_ref[0]
    lhs = jnp.where(first, zr2 * btr - zi2 * bti, -(zr2 * bti + zi2 * btr))

    lh, ll = lhs.astype(BF16), (lhs - lhs.astype(BF16).astype(F32)).astype(BF16)
    rh, rl = rhs.astype(BF16), (rhs - rhs.astype(BF16).astype(F32)).astype(BF16)
    z = (jnp.dot(lh, rh, preferred_element_type=F32)
         + jnp.dot(lh, rl, preferred_element_type=F32)
         + jnp.dot(ll, rh, preferred_element_type=F32))
    lane_z = lax.broadcasted_iota(jnp.int32, (hh, 2 * tw), 1)
    row_z = lax.broadcasted_iota(jnp.int32, (hh, 2 * tw), 0)
    z = z + jnp.where((lane_z // hh == t) & (lane_z % hh == row_z), d_ref[0], 0.0)
    for s in range(t):
        off = (t - s) * hh
        m_ref[s * hh:(s + 1) * hh, :] = pltpu.roll(z, 2 * tw - off, axis=1)[:, :tw].astype(BF16)

    u = u_ref[0]
    y = jnp.dot(u, m_ref[...], preferred_element_type=F32)
    x = lax.dot_general(u, q_t, (((1,), (1,)), ((), ())), preferred_element_type=F32)

    sqc = _squarings(ar2, ai2, n_tbits + n_cbits - 1)[n_tbits - 1:]
    sign = jnp.where(first, -1.0, 1.0)
    ridx = lax.broadcasted_iota(jnp.int32, (nc, 2 * p), 0)

    def shift_down(v, k):
        if k % 8 == 0:
            return jnp.concatenate([jnp.zeros((k, 2 * p), F32), v[:nc - k]], axis=0)
        return jnp.where(ridx >= k, pltpu.roll(v, k, axis=0), 0.0)

    parts = []
    for b in range(n_batch):
        sb = x[b * nc:(b + 1) * nc]
        for k in range(n_cbits):
            pr_k, pi_k = sqc[k]
            sh = shift_down(sb, 1 << k)
            sb = sb + pr_k * sh + (sign * pi_k) * pltpu.roll(sh, p, axis=1)
        parts.append(shift_down(sb, 1))
    s_prev = jnp.concatenate(parts, axis=0)
    y = y + jnp.dot(s_prev.astype(BF16), p_m, preferred_element_type=F32)
    y_ref[0] = y


def _s5_mix(u, lam_re, lam_im, log_dt, b_re, b_im, c_re, c_im, d_skip, n_batch):
    m, width = u.shape
    g, p, hh = b_re.shape
    t = S5_CHUNK
    tw = t * hh
    seq = m // n_batch
    nc = seq // t
    rows = n_batch * nc
    ug = u.reshape(n_batch, nc, t, g, hh).transpose(3, 0, 1, 2, 4).reshape(g, rows, tw)

    col = lambda a: a.reshape(g, p, 1)
    row2 = lambda a: jnp.concatenate([a, a], axis=-1).reshape(g, 1, 2 * p)
    b_re_t = jnp.tile(b_re, (1, 1, t))
    b_im_t = jnp.tile(b_im, (1, 1, t))
    ct_re_t = jnp.tile(c_re.transpose(0, 2, 1), (1, 1, 2 * t))
    ct_im_t = jnp.tile(c_im.transpose(0, 2, 1), (1, 1, 2 * t))
    bt_re2 = jnp.tile(b_re.transpose(0, 2, 1), (1, 1, 2))
    bt_im2 = jnp.tile(b_im.transpose(0, 2, 1), (1, 1, 2))
    d_t = jnp.tile(d_skip, (1, 2 * t)).reshape(g, 1, 2 * tw)

    gspec = lambda *shape: pl.BlockSpec((1,) + shape, lambda i: (i,) + (0,) * len(shape))
    kern = functools.partial(_s5_kernel, n_batch=n_batch, chunk=t, n_state=p, n_chan=hh)
    yg = pl.pallas_call(
        kern,
        grid=(g,),
        in_specs=[gspec(rows, tw),
                  gspec(p, 1), gspec(p, 1), gspec(1, 2 * p), gspec(1, 2 * p), gspec(1, 1),
                  gspec(p, tw), gspec(p, tw), gspec(p, 2 * tw), gspec(p, 2 * tw),
                  gspec(hh, 2 * p), gspec(hh, 2 * p), gspec(1, 2 * tw)],
        out_specs=gspec(rows, tw),
        out_shape=jax.ShapeDtypeStruct((g, rows, tw), F32),
        scratch_shapes=[pltpu.VMEM((tw, tw), BF16)],
        compiler_params=_cparams("parallel"),
        name="s5_mix",
    )(ug, col(lam_re), col(lam_im), row2(lam_re), row2(lam_im), log_dt.reshape(g, 1, 1),
      b_re_t, b_im_t, ct_re_t, ct_im_t, bt_re2, bt_im2, d_t)
    return yg.reshape(g, n_batch, nc, t, hh).transpose(1, 2, 3, 0, 4).reshape(m, width)


HG_LOW_LEVELS = 3


def _hgrn_kernel(x_ref, lbl_ref, hn_ref, o_ref, st_ref, *, layer, n_heads, chunk):
    c = chunk
    n_sub = x_ref.shape[0] // c
    width = o_ref.shape[-1]
    dk = width // n_heads
    n_lev = c.bit_length() - 1

    @pl.when(pl.program_id(1) == 0)
    def _():
        st_ref[...] = jnp.zeros_like(st_ref)

    lg = lbl_ref[...]
    e = jnp.exp(lg - jnp.max(lg, axis=0, keepdims=True))
    sm = e / jnp.sum(e, axis=0, keepdims=True)
    lb = jnp.zeros((1, width), F32)
    for r in range(1, layer + 1):
        lb = lb + sm[r:r + 1]
    lb_floor = jnp.maximum(lb, 1e-30)
    one_m_lb = 1.0 - lb
    hnw = hn_ref[...]

    ri = lax.broadcasted_iota(jnp.int32, (c, c), 0)
    ci = lax.broadcasted_iota(jnp.int32, (c, c), 1)
    tri = jnp.where(ci <= ri, 1.0, 0.0).astype(BF16)
    keeps, lows = [], []
    for l in range(1, n_lev + 1):
        blk, half = 1 << l, 1 << (l - 1)
        keeps.append((ri // blk == ci // blk) & (ri % blk >= half) & (ci % blk < half))
        if l <= HG_LOW_LEVELS:
            bd = (ri // blk) * blk + half - 1
            lows.append(jnp.where(((ci > bd) & (ci <= ri)) | ((ci > ri) & (ci <= bd)), 1.0, 0.0))
    w_low = jnp.concatenate(lows, axis=0).astype(BF16)
    diag = ri == ci

    nt = (((1,), (1,)), ((), ()))
    tn = (((0,), (0,)), ((), ()))
    for sub in range(n_sub):
        rows = slice(sub * c, (sub + 1) * c)
        q = x_ref[rows, 0:width]
        zf = x_ref[rows, width:2 * width]
        v = x_ref[rows, 2 * width:3 * width]
        gate = x_ref[rows, 3 * width:4 * width]

        ez = jnp.exp(-jnp.abs(zf))
        rz = 1.0 / (1.0 + ez)
        pos = zf >= 0.0
        sig_p = jnp.where(pos, rz, ez * rz)
        sig_n = jnp.where(pos, ez * rz, rz)
        logf = jnp.log(lb_floor + one_m_lb * sig_p)
        k = one_m_lb * sig_n

        hi, mid, lo = _split3(logf)
        b = (jnp.dot(tri, hi, preferred_element_type=F32)
             + jnp.dot(tri, mid, preferred_element_type=F32)
             + jnp.dot(tri, lo, preferred_element_type=F32))
        ex_low = (jnp.dot(w_low, hi, preferred_element_type=F32)
                  + jnp.dot(w_low, mid, preferred_element_type=F32))

        att = [jnp.where(diag, jnp.sum((q * k)[:, h * dk:(h + 1) * dk], axis=-1, keepdims=True), 0.0)
               for h in range(n_heads)]
        for l in range(1, n_lev + 1):
            blk, half = 1 << l, 1 << (l - 1)
            if l <= HG_LOW_LEVELS:
                ex = ex_low[(l - 1) * c:l * c]
            else:
                pieces = []
                for s0 in range(0, c, blk):
                    bd = b[s0 + half - 1:s0 + half]
                    pieces.append(bd - b[s0:s0 + half])
                    pieces.append(b[s0 + half:s0 + blk] - bd)
                ex = jnp.concatenate(pieces, axis=0)
            wgt = jnp.exp(ex)
            ql, kl = (q * wgt).astype(BF16), (k * wgt).astype(BF16)
            for h in range(n_heads):
                sl = slice(h * dk, (h + 1) * dk)
                pm = lax.dot_general(ql[:, sl], kl[:, sl], nt, preferred_element_type=F32)
                att[h] = att[h] + jnp.where(keeps[l - 1], pm, 0.0)

        b_last = b[c - 1:c]
        q_in = (q * jnp.exp(b)).astype(BF16)
        k_out = (k * jnp.exp(b_last - b)).astype(BF16)
        e_last = jnp.exp(b_last)
        vb = v.astype(BF16)
        for h in range(n_heads):
            sl = slice(h * dk, (h + 1) * dk)
            st = st_ref[h]
            o = lax.dot_general(q_in[:, sl], st.astype(BF16), nt, preferred_element_type=F32)
            o = o + jnp.dot(att[h].astype(BF16), vb[:, sl], preferred_element_type=F32)
            st_ref[h] = st * e_last[:, sl] + lax.dot_general(vb[:, sl], k_out[:, sl], tn,
                                                              preferred_element_type=F32)
            o = o * lax.rsqrt(jnp.mean(o * o, axis=-1, keepdims=True) + EPS) * hnw[:, sl]
            gh = gate[:, sl]
            o_ref[rows, sl] = o * (gh * _sigmoid(gh))


def _hgrn_mix(x, lb_logits, head_norm, layer, n_batch):
    m = x.shape[0]
    n_layers, width = lb_logits.shape
    n_heads = head_norm.shape[0]
    dk = width // n_heads
    rows = HG_BLOCK
    per = m // n_batch // rows
    kern = functools.partial(_hgrn_kernel, layer=layer, n_heads=n_heads, chunk=HG_CHUNK)
    return pl.pallas_call(
        kern,
        grid=(n_batch, per),
        in_specs=[pl.BlockSpec((rows, 4 * width), lambda b, i: (b * per + i, 0)),
                  _const_spec((n_layers, width)),
                  _const_spec((1, width))],
        out_specs=pl.BlockSpec((rows, width), lambda b, i: (b * per + i, 0)),
        out_shape=jax.ShapeDtypeStruct((m, width), F32),
        scratch_shapes=[pltpu.VMEM((n_heads, dk, dk), F32)],
        compiler_params=_cparams("parallel", "arbitrary"),
        name="hgrn_mix",
    )(x, lb_logits, head_norm.reshape(1, width))


_STACK = (0, 2, 1, 3)


def _attn_kernel(sink_ref, q_ref, kv_ref, kvp_ref, o_ref, bias_ref, sinkc_ref, *, n_q, n_kv):
    tq = q_ref.shape[0]
    hd = q_ref.shape[1] // n_q
    grp = n_q // n_kv
    w = WINDOW
    lanes = 2 * hd
    nkw = n_kv * lanes
    first_block = pl.program_id(1) == 0

    @pl.when((pl.program_id(0) == 0) & first_block)
    def _():
        rr = lax.broadcasted_iota(jnp.int32, (2 * w, grp * w), 0)
        cc = lax.broadcasted_iota(jnp.int32, (2 * w, grp * w), 1)
        dist = cc % w + w - rr
        valid = (dist >= 0) & (dist < w)
        distf = dist.astype(F32)
        pos = cc // w
        posr = lax.broadcasted_iota(jnp.int32, (8, grp * w), 1) // w
        for kh in range(n_kv):
            slope = jnp.zeros((2 * w, grp * w), F32)
            sink = jnp.zeros((8, grp * w), F32)
            for i, g in enumerate(_STACK):
                head = kh * grp + g
                slope = jnp.where(pos == i, _LOG2E * 2.0 ** (-8.0 * (head + 1) / n_q), slope)
                sink = jnp.where(posr == i, _LOG2E * sink_ref[head], sink)
            b = jnp.where(valid, -slope * distf, NEG_BIG)
            bias_ref[0, kh] = b
            bias_ref[1, kh] = jnp.where(rr >= w, b, NEG_BIG)
            sinkc_ref[kh] = sink

    lane_q = lax.broadcasted_iota(jnp.int32, (w, lanes), 1)
    lo_q = lane_q < hd
    lane_v = lax.broadcasted_iota(jnp.int32, (2 * w, lanes), 1)
    lo_v = lane_v < hd
    zq = jnp.zeros((w, lanes), BF16)
    zv = jnp.zeros((2 * w, lanes), BF16)
    first_idx = jnp.where(first_block, 1, 0)

    nt = (((1,), (1,)), ((), ()))
    tn = (((0,), (0,)), ((), ()))

    def scores(jb, kh):
        rs = slice(jb * w, (jb + 1) * w)
        ks = slice(kh * lanes, (kh + 1) * lanes)
        if jb == 0:
            kprev = kvp_ref[:, ks]
            bias = bias_ref[first_idx, kh]
        else:
            kprev = kv_ref[(jb - 1) * w:jb * w, ks]
            bias = bias_ref[0, kh]
        kw = jnp.concatenate([kprev, kv_ref[rs, ks]], axis=0)
        qa = q_ref[rs, (2 * kh) * lanes:(2 * kh + 1) * lanes]
        qb = q_ref[rs, (2 * kh + 1) * lanes:(2 * kh + 2) * lanes]
        qs = jnp.concatenate([jnp.where(lo_q, qa, zq), jnp.where(lo_q, qb, zq),
                              jnp.where(lo_q, zq, qa), jnp.where(lo_q, zq, qb)], axis=0)
        return lax.dot_general(kw, qs, nt, preferred_element_type=F32) + bias

    def finish(jb, kh, st):
        rs = slice(jb * w, (jb + 1) * w)
        vs = slice(nkw + kh * lanes, nkw + (kh + 1) * lanes)
        vprev = kvp_ref[:, vs] if jb == 0 else kv_ref[(jb - 1) * w:jb * w, vs]
        vw = jnp.concatenate([vprev, kv_ref[rs, vs]], axis=0)
        sink = sinkc_ref[kh][0:1]
        mx = jnp.maximum(jnp.max(st, axis=0, keepdims=True), sink)
        pe = jnp.exp2(st - mx)
        den = jnp.sum(pe, axis=0, keepdims=True) + jnp.exp2(sink - mx)
        pn = (pe * (1.0 / den)).astype(BF16)
        o = (lax.dot_general(pn[:, 0:2 * w], jnp.where(lo_v, vw, zv), tn, preferred_element_type=F32)
             + lax.dot_general(pn[:, 2 * w:4 * w], jnp.where(lo_v, zv, vw), tn, preferred_element_type=F32))
        o_ref[rs, (2 * kh) * lanes:(2 * kh + 1) * lanes] = o[0:w].astype(o_ref.dtype)
        o_ref[rs, (2 * kh + 1) * lanes:(2 * kh + 2) * lanes] = o[w:2 * w].astype(o_ref.dtype)

    items = [(jb, kh) for jb in range(tq // w) for kh in range(n_kv)]
    pending = [scores(*items[i]) for i in range(ATTN_AHEAD)]
    for idx, item in enumerate(items):
        st = pending.pop(0)
        if idx + ATTN_AHEAD < len(items):
            pending.append(scores(*items[idx + ATTN_AHEAD]))
        finish(*item, st)


def _attention(q, kv, sinks, n_batch):
    m, qw = q.shape
    n_q = sinks.shape[0]
    hd = qw // n_q
    n_kv = kv.shape[1] // (4 * hd)
    grp = n_q // n_kv
    assert grp == 4 and 2 * hd == 128
    tq = ATTN_BLOCK_Q
    per = m // n_batch // tq
    ratio = tq // WINDOW
    kern = functools.partial(_attn_kernel, n_q=n_q, n_kv=n_kv)
    return pl.pallas_call(
        kern,
        grid=(n_batch, per),
        in_specs=[pl.BlockSpec(memory_space=pltpu.SMEM),
                  pl.BlockSpec((tq, qw), lambda b, i: (b * per + i, 0)),
                  pl.BlockSpec((tq, kv.shape[1]), lambda b, i: (b * per + i, 0)),
                  pl.BlockSpec((WINDOW, kv.shape[1]),
                               lambda b, i: (jnp.maximum((b * per + i) * ratio - 1, 0), 0))],
        out_specs=pl.BlockSpec((tq, qw), lambda b, i: (b * per + i, 0)),
        out_shape=jax.ShapeDtypeStruct((m, qw), BF16),
        scratch_shapes=[pltpu.VMEM((2, n_kv, 2 * WINDOW, grp * WINDOW), F32),
                        pltpu.VMEM((n_kv, 8, grp * WINDOW), F32)],
        compiler_params=_cparams("arbitrary", "arbitrary"),
        name="swa_attention",
    )(sinks, q, kv, kv)


def _dup_heads(wcols, n_heads):
    d = wcols.shape[0]
    wh = wcols.reshape(d, n_heads, -1)
    return jnp.concatenate([wh, wh], axis=-1).reshape(d, -1)


def _tail_kernel(*refs, even, final, ff_block):
    refs = list(refs)
    o_ref = refs.pop()
    h_ref = refs.pop(0)
    if even:
        ya_ref, yb_ref, wglu_ref, bglu_ref = refs[:4]
        refs = refs[4:]
    else:
        a_ref = refs.pop(0)
    p_ref, wout_ref, g2_ref, w1_ref, w2_ref, g3_ref, wup_ref, wgate_ref = refs[:8]
    gf_ref = refs[8] if final else None

    h = h_ref[...]
    if even:
        z = jax.nn.gelu(ya_ref[...])
        oa = z * _sigmoid(_bdot(z, wglu_ref[...]) + bglu_ref[...])
        wa = oa.shape[-1]
        mix = (_bdot(oa, wout_ref[0:wa, :]) + _bdot(yb_ref[...], wout_ref[wa:, :]))
    else:
        mix = _bdot(a_ref[...], wout_ref[...])
    h = h + mix

    hn = _rms(h, g2_ref[...]).astype(BF16)
    d_ff = w1_ref.shape[1]
    acc = jnp.zeros_like(h)
    for c0 in range(0, d_ff, ff_block):
        a = jnp.dot(hn, w1_ref[:, c0:c0 + ff_block], preferred_element_type=F32)
        a = jnp.square(jnp.maximum(a, 0.0)).astype(BF16)
        acc = acc + jnp.dot(a, w2_ref[c0:c0 + ff_block, :], preferred_element_type=F32)
    h = h + acc

    hn = _rms(h, g3_ref[...])
    gate = _sigmoid(_bdot(hn, wgate_ref[...]))
    h = h + _bdot(p_ref[...], wup_ref[...]) * gate
    if final:
        h = _rms(h, gf_ref[...])
    o_ref[...] = h


def _tail(h, mixer_ins, mixer_weights, p, w_out, g2, w1, w2, g3, w_up, w_gate, g_final, even):
    m, d = h.shape
    tm = TOKEN_BLOCK
    tok = lambda a: pl.BlockSpec((tm, a.shape[1]), lambda i: (i, 0))
    rowv = lambda a: a.reshape(1, -1)
    weights = [w_out, rowv(g2), w1, w2, rowv(g3), w_up, w_gate]
    final = g_final is not None
    if final:
        weights.append(rowv(g_final))
    args = [h, *mixer_ins, *mixer_weights, p, *weights]
    specs = ([tok(h)] + [tok(a) for a in mixer_ins] + [_const_spec(a.shape) for a in mixer_weights]
             + [tok(p)] + [_const_spec(a.shape) for a in weights])
    kern = functools.partial(_tail_kernel, even=even, final=final, ff_block=1024)
    return pl.pallas_call(
        kern,
        grid=(m // tm,),
        in_specs=specs,
        out_specs=pl.BlockSpec((tm, d), lambda i: (i, 0)),
        out_shape=jax.ShapeDtypeStruct((m, d), F32),
        compiler_params=_cparams("parallel"),
        name="layer_tail",
    )(*args)


def kernel(x, p, mix_norm, mlp_norm, ple_norm, final_norm, w_in_even, w_out_even, s5_lam_re, s5_lam_im, s5_log_dt, s5_b_re, s5_b_im, s5_c_re, s5_c_im, s5_d, s5_w_glu, s5_b_glu, hgrn_lb_logits, hgrn_norm, w_qkv_odd, w_o_odd, attn_sinks, w_mlp_in, w_mlp_out, w_ple_up, w_ple_gate):
    bsz, seq, d = x.shape
    depth = p.shape[0]
    m = bsz * seq
    s5_w = s5_w_glu.shape[-1]
    hg_w = hgrn_lb_logits.shape[-1]
    n_q = attn_sinks.shape[-1]
    hd = d // n_q
    kv_w = w_qkv_odd.shape[-1] - n_q * hd
    assert seq % max(TOKEN_BLOCK, ATTN_BLOCK_Q, HG_BLOCK, S5_CHUNK) == 0

    h = x.reshape(m, d)
    for i in range(depth):
        j = i // 2
        if i % 2 == 0:
            u, hg = _norm_proj(h, mix_norm[i], w_in_even[j].astype(BF16),
                               [(s5_w, BF16, 1.0), (4 * hg_w, F32, 1.0)])
            ya = _s5_mix(u, s5_lam_re[j], s5_lam_im[j], s5_log_dt[j], s5_b_re[j], s5_b_im[j],
                         s5_c_re[j], s5_c_im[j], s5_d[j], bsz)
            yb = _hgrn_mix(hg, hgrn_lb_logits, hgrn_norm[j], j, bsz)
            mixer_ins = [ya, yb]
            mixer_weights = [s5_w_glu[j].astype(BF16), s5_b_glu[j].reshape(1, -1)]
            w_out = w_out_even[j]
        else:
            wq, wk, wv = jnp.split(w_qkv_odd[j], [n_q * hd, n_q * hd + kv_w // 2], axis=-1)
            n_kv = kv_w // (2 * hd)
            w_qkv = jnp.concatenate([wq, _dup_heads(wk, n_kv), _dup_heads(wv, n_kv)], axis=-1)
            q, kv = _norm_proj(h, mix_norm[i], w_qkv.astype(BF16),
                               [(n_q * hd, BF16, _LOG2E / math.sqrt(hd)), (2 * kv_w, BF16, 1.0)])
            mixer_ins = [_attention(q, kv, attn_sinks[j], bsz)]
            mixer_weights = []
            w_out = w_o_odd[j]
        h = _tail(h, mixer_ins, mixer_weights, p[i].reshape(m, -1), w_out.astype(BF16),
                  mlp_norm[i], w_mlp_in[i].astype(BF16), w_mlp_out[i].astype(BF16),
                  ple_norm[i], w_ple_up[i].astype(BF16), w_ple_gate[i].astype(BF16),
                  final_norm if i == depth - 1 else None, even=(i % 2 == 0))
    return h.reshape(bsz, seq, d)
```

```python
import functools
import math

import jax
import jax.numpy as jnp
from jax import lax
from jax.experimental import pallas as pl
from jax.experimental.pallas import tpu as pltpu

F32 = jnp.float32
BF16 = jnp.bfloat16

EPS = 1e-6
WINDOW = 128
S5_CHUNK = 16
LANES = 128
HG_CHUNK = 128
HG_BLOCK = 512
LAM_RE_MAX = -1e-4
NEG_BIG = -1e30
_LOG2E = math.log2(math.e)

_V7X_VMEM_BYTES = 64 * 1024 * 1024
_VMEM_LIMIT = _V7X_VMEM_BYTES - 12 * 1024 * 1024

TOKEN_BLOCK = 512
ATTN_BLOCK_Q = 512
ATTN_AHEAD = 2


def _cparams(*sem):
    return pltpu.CompilerParams(dimension_semantics=sem, vmem_limit_bytes=_VMEM_LIMIT)


def _const_spec(shape):
    nd = len(shape)
    return pl.BlockSpec(shape, lambda *_: (0,) * nd, pipeline_mode=pl.Buffered(1))


def _rms(x, g):
    ms = jnp.mean(x * x, axis=-1, keepdims=True)
    return x * lax.rsqrt(ms + EPS) * g


def _sigmoid(x):
    return 1.0 / (1.0 + jnp.exp(-x))


def _bdot(a, b):
    return jnp.dot(a.astype(BF16), b.astype(BF16), preferred_element_type=F32)


def _split3(x):
    hi = x.astype(BF16)
    r = x - hi.astype(F32)
    mid = r.astype(BF16)
    lo = (r - mid.astype(F32)).astype(BF16)
    return hi, mid, lo


def _to_chunk_rows(val, o_ref, scr_ref, chunk):
    n_tiles = scr_ref.shape[0]
    rows = o_ref.shape[0]
    for k in range(n_tiles):
        scr_ref[k] = val[:, k * LANES:(k + 1) * LANES]
    for k in range(n_tiles):
        for t in range(chunk):
            o_ref[:, (k * chunk + t) * LANES:(k * chunk + t + 1) * LANES] = scr_ref[k, pl.ds(t, rows, stride=chunk), :]


def _from_chunk_rows(x_ref, scr_ref, chunk):
    n_tiles = scr_ref.shape[0]
    rows = x_ref.shape[0]
    for k in range(n_tiles):
        for t in range(chunk):
            scr_ref[k, pl.ds(t, rows, stride=chunk), :] = x_ref[:, (k * chunk + t) * LANES:(k * chunk + t + 1) * LANES]
    return jnp.concatenate([scr_ref[k] for k in range(n_tiles)], axis=1)


def _proj_kernel(h_ref, g_ref, w_ref, *refs, scales, widths, chunk_rows):
    out_refs = refs[:len(scales)]
    hn = _rms(h_ref[...], g_ref[...]).astype(BF16)
    off = 0
    for i, (o_ref, s, n) in enumerate(zip(out_refs, scales, widths)):
        acc = jnp.dot(hn, w_ref[:, off:off + n], preferred_element_type=F32)
        if s != 1.0:
            acc = acc * s
        if i == 0 and chunk_rows:
            _to_chunk_rows(acc, o_ref, refs[-1], chunk_rows)
        else:
            o_ref[...] = acc.astype(o_ref.dtype)
        off += n


def _norm_proj(h, g, w, outs, chunk_rows=0):
    m, d = h.shape
    tm = TOKEN_BLOCK
    widths = tuple(n for n, _, _ in outs)
    kern = functools.partial(_proj_kernel, scales=tuple(s for _, _, s in outs), widths=widths,
                             chunk_rows=chunk_rows)
    out_specs = [pl.BlockSpec((tm, n), lambda i: (i, 0)) for n in widths]
    out_shape = [jax.ShapeDtypeStruct((m, n), dt) for n, dt, _ in outs]
    scratch = []
    if chunk_rows:
        n0 = widths[0]
        out_specs[0] = pl.BlockSpec((tm // chunk_rows, n0 * chunk_rows), lambda i: (i, 0))
        out_shape[0] = jax.ShapeDtypeStruct((m // chunk_rows, n0 * chunk_rows), outs[0][1])
        scratch = [pltpu.VMEM((n0 // LANES, tm, LANES), F32)]
    return pl.pallas_call(
        kern,
        grid=(m // tm,),
        in_specs=[pl.BlockSpec((tm, d), lambda i: (i, 0)),
                  _const_spec((1, d)),
                  _const_spec(w.shape)],
        out_specs=out_specs,
        out_shape=out_shape,
        scratch_shapes=scratch,
        compiler_params=_cparams("parallel"),
        name="norm_proj",
    )(h, g.reshape(1, d), w)


def _cmul(ar, ai, br, bi):
    return ar * br - ai * bi, ar * bi + ai * br


def _s5_discretize(lam_re, lam_im, dt):
    lr = jnp.minimum(lam_re, LAM_RE_MAX)
    li = lam_im
    mag = jnp.exp(lr * dt)
    ar = mag * jnp.cos(li * dt)
    ai = mag * jnp.sin(li * dt)
    den = lr * lr + li * li
    xr = ar - 1.0
    zr = (xr * lr + ai * li) / den
    zi = (ai * lr - xr * li) / den
    return ar, ai, zr, zi


def _s5_kernel(u_ref, lam_re_ref, lam_im_ref, logdt_ref, ct_re_ref, ct_im_ref, b_re_ref, b_im_ref,
               lamr_re_ref, lamr_im_ref, logdtr_ref, bt_re_ref, bt_im_ref, d_ref,
               lams_re_ref, lams_im_ref, logdts_ref,
               y_ref, m_ref, qt_ref, pm_ref, *, chunk, n_chan):
    t, hh = chunk, n_chan
    tw = t * LANES
    p = lam_re_ref.shape[1]
    gt = LANES // hh
    sw = gt * p
    nc = u_ref.shape[0]
    n_cbits = max(1, (nc - 1).bit_length())

    @pl.when(pl.program_id(1) == 0)
    def _():
        dt = jnp.exp(logdt_ref[0])
        ar, ai, zr, zi = _s5_discretize(lam_re_ref[0], lam_im_ref[0], dt)
        pw = [(jnp.ones((p, LANES), F32), jnp.zeros((p, LANES), F32))]
        for _ in range(t):
            pw.append(_cmul(pw[-1][0], pw[-1][1], ar, ai))
        bbr, bbi = _cmul(zr, zi, b_re_ref[0], b_im_ref[0])
        ctr, cti = ct_re_ref[0], ct_im_ref[0]
        lane_g = lax.broadcasted_iota(jnp.int32, (p, LANES), 1) // hh
        zero = jnp.zeros((p, LANES), F32)
        q_slots = [_cmul(pw[t - 1 - s][0], pw[t - 1 - s][1], bbr, bbi) for s in range(t)]
        c_slots = [_cmul(ctr, cti, pw[s + 1][0], pw[s + 1][1]) for s in range(t)]
        for g2 in range(gt):
            own = lane_g == g2
            rows_re = slice(g2 * p, (g2 + 1) * p)
            rows_im = slice(sw + g2 * p, sw + (g2 + 1) * p)
            qt_ref[rows_re, :] = jnp.concatenate([jnp.where(own, r, zero) for r, _ in q_slots], axis=1).astype(BF16)
            qt_ref[rows_im, :] = jnp.concatenate([jnp.where(own, i, zero) for _, i in q_slots], axis=1).astype(BF16)
            pm_ref[rows_re, :] = jnp.concatenate([jnp.where(own, r, zero) for r, _ in c_slots], axis=1).astype(BF16)
            pm_ref[rows_im, :] = jnp.concatenate([jnp.where(own, -i, zero) for _, i in c_slots], axis=1).astype(BF16)

        _, _, zr2, zi2 = _s5_discretize(lamr_re_ref[0], lamr_im_ref[0], jnp.exp(logdtr_ref[0]))
        first = lax.broadcasted_iota(jnp.int32, (LANES, 2 * p), 1) < p
        btr, bti = bt_re_ref[0], bt_im_ref[0]
        lhs = jnp.where(first, zr2 * btr - zi2 * bti, -(zr2 * bti + zi2 * btr))
        rhs = jnp.concatenate(
            [jnp.concatenate([_cmul(ctr, cti, pw[tau][0], pw[tau][1])[0] for tau in range(t)], axis=1),
             jnp.concatenate([_cmul(ctr, cti, pw[tau][0], pw[tau][1])[1] for tau in range(t)], axis=1)],
            axis=0)
        lh, ll = lhs.astype(BF16), (lhs - lhs.astype(BF16).astype(F32)).astype(BF16)
        rh, rl = rhs.astype(BF16), (rhs - rhs.astype(BF16).astype(F32)).astype(BF16)
        z = (jnp.dot(lh, rh, preferred_element_type=F32)
             + jnp.dot(lh, rl, preferred_element_type=F32)
             + jnp.dot(ll, rh, preferred_element_type=F32))
        row_z = lax.broadcasted_iota(jnp.int32, (LANES, tw), 0)
        lane_z = lax.broadcasted_iota(jnp.int32, (LANES, tw), 1)
        z = jnp.where(row_z // hh == (lane_z % LANES) // hh, z, 0.0)
        d_diag = jnp.where(lax.broadcasted_iota(jnp.int32, (LANES, LANES), 0)
                           == lax.broadcasted_iota(jnp.int32, (LANES, LANES), 1), d_ref[0], 0.0)
        z = jnp.concatenate([z[:, 0:LANES] + d_diag, z[:, LANES:]], axis=1)
        zb = z.astype(BF16)
        for s_in in range(t):
            if s_in:
                m_ref[s_in * LANES:(s_in + 1) * LANES, 0:s_in * LANES] = jnp.zeros((LANES, s_in * LANES), BF16)
            m_ref[s_in * LANES:(s_in + 1) * LANES, s_in * LANES:tw] = zb[:, 0:tw - s_in * LANES]

    u = u_ref[...].astype(BF16)
    y = jnp.concatenate(
        [jnp.dot(u[:, 0:(i + 2) * LANES], m_ref[0:(i + 2) * LANES, i * LANES:(i + 2) * LANES],
                 preferred_element_type=F32) for i in range(0, t, 2)], axis=1)
    x = lax.dot_general(u, qt_ref[...], (((1,), (1,)), ((), ())), preferred_element_type=F32)

    mr, mi, _, _ = _s5_discretize(lams_re_ref[0], lams_im_ref[0], jnp.exp(logdts_ref[0]))
    for _ in range(t.bit_length() - 1):
        mr, mi = _cmul(mr, mi, mr, mi)
    ridx = lax.broadcasted_iota(jnp.int32, (nc, sw), 0)

    def shift_down(v, k):
        if k % 8 == 0:
            return jnp.concatenate([jnp.zeros((k, sw), F32), v[:nc - k]], axis=0)
        return jnp.where(ridx >= k, pltpu.roll(v, k, axis=0), 0.0)

    s_re, s_im = x[:, 0:sw], x[:, sw:2 * sw]
    for k in range(n_cbits):
        h_re, h_im = shift_down(s_re, 1 << k), shift_down(s_im, 1 << k)
        s_re, s_im = s_re + (mr * h_re - mi * h_im), s_im + (mr * h_im + mi * h_re)
        mr, mi = _cmul(mr, mi, mr, mi)
    s_prev = jnp.concatenate([shift_down(s_re, 1), shift_down(s_im, 1)], axis=1).astype(BF16)
    y_ref[...] = y + jnp.dot(s_prev, pm_ref[...], preferred_element_type=F32)


def _s5_mix(u, lam_re, lam_im, log_dt, b_re, b_im, c_re, c_im, d_skip, n_batch):
    rows, _ = u.shape
    g, p, hh = b_re.shape
    t = S5_CHUNK
    gt = LANES // hh
    nt = g // gt
    tw = t * LANES
    nc = rows // n_batch

    def lanes_gh(a):
        return a.reshape(nt, gt, p, hh).transpose(0, 2, 1, 3).reshape(nt, p, LANES)

    def rows_gh(a):
        r = a.reshape(nt, gt, p, hh).transpose(0, 1, 3, 2).reshape(nt, LANES, p)
        return jnp.concatenate([r, r], axis=-1)

    per_gh = lambda a: jnp.broadcast_to(a.reshape(g, p, 1), (g, p, hh))
    lam_re_c, lam_im_c = lanes_gh(per_gh(lam_re)), lanes_gh(per_gh(lam_im))
    logdt_c = jnp.broadcast_to(log_dt.reshape(nt, gt, 1), (nt, gt, hh)).reshape(nt, 1, LANES)
    ct_re, ct_im = lanes_gh(c_re.transpose(0, 2, 1)), lanes_gh(c_im.transpose(0, 2, 1))
    b_re_c, b_im_c = lanes_gh(b_re), lanes_gh(b_im)
    lamr = lambda a: rows_gh(jnp.broadcast_to(a.reshape(g, p, 1), (g, p, hh)))
    logdt_r = jnp.broadcast_to(log_dt.reshape(nt, gt, 1), (nt, gt, hh)).reshape(nt, LANES, 1)
    d_c = d_skip.reshape(nt, 1, LANES)
    lams = lambda a: a.reshape(nt, 1, gt * p)
    logdt_s = jnp.broadcast_to(log_dt.reshape(nt, gt, 1), (nt, gt, p)).reshape(nt, 1, gt * p)

    params = [lam_re_c, lam_im_c, logdt_c, ct_re, ct_im, b_re_c, b_im_c,
              lamr(lam_re), lamr(lam_im), logdt_r, rows_gh(b_re), rows_gh(b_im), d_c,
              lams(lam_re), lams(lam_im), logdt_s]
    pspec = lambda a: pl.BlockSpec((1,) + a.shape[1:], lambda k, b: (k,) + (0,) * (a.ndim - 1))
    kern = functools.partial(_s5_kernel, chunk=t, n_chan=hh)
    return pl.pallas_call(
        kern,
        grid=(nt, n_batch),
        in_specs=[pl.BlockSpec((nc, tw), lambda k, b: (b, k))] + [pspec(a) for a in params],
        out_specs=pl.BlockSpec((nc, tw), lambda k, b: (b, k)),
        out_shape=jax.ShapeDtypeStruct(u.shape, F32),
        scratch_shapes=[pltpu.VMEM((tw, tw), BF16),
                        pltpu.VMEM((2 * gt * p, tw), BF16),
                        pltpu.VMEM((2 * gt * p, tw), BF16)],
        compiler_params=_cparams("arbitrary", "arbitrary"),
        name="s5_mix",
    )(u, *params)


HG_LOW_LEVELS = 3


def _hgrn_kernel(x_ref, lbl_ref, hn_ref, o_ref, st_ref, *, layer, n_heads, chunk):
    c = chunk
    n_sub = x_ref.shape[0] // c
    width = o_ref.shape[-1]
    dk = width // n_heads
    n_lev = c.bit_length() - 1

    @pl.when(pl.program_id(1) == 0)
    def _():
        st_ref[...] = jnp.zeros_like(st_ref)

    lg = lbl_ref[...]
    e = jnp.exp(lg - jnp.max(lg, axis=0, keepdims=True))
    sm = e / jnp.sum(e, axis=0, keepdims=True)
    lb = jnp.zeros((1, width), F32)
    for r in range(1, layer + 1):
        lb = lb + sm[r:r + 1]
    lb_floor = jnp.maximum(lb, 1e-30)
    one_m_lb = 1.0 - lb
    hnw = hn_ref[...]

    ri = lax.broadcasted_iota(jnp.int32, (c, c), 0)
    ci = lax.broadcasted_iota(jnp.int32, (c, c), 1)
    tri = jnp.where(ci <= ri, 1.0, 0.0).astype(BF16)
    keeps, lows = [], []
    for l in range(1, n_lev + 1):
        blk, half = 1 << l, 1 << (l - 1)
        keeps.append((ri // blk == ci // blk) & (ri % blk >= half) & (ci % blk < half))
        if l <= HG_LOW_LEVELS:
            bd = (ri // blk) * blk + half - 1
            lows.append(jnp.where(((ci > bd) & (ci <= ri)) | ((ci > ri) & (ci <= bd)), 1.0, 0.0))
    w_low = jnp.concatenate(lows, axis=0).astype(BF16)
    diag = ri == ci

    nt = (((1,), (1,)), ((), ()))
    tn = (((0,), (0,)), ((), ()))
    for sub in range(n_sub):
        rows = slice(sub * c, (sub + 1) * c)
        q = x_ref[rows, 0:width]
        zf = x_ref[rows, width:2 * width]
        v = x_ref[rows, 2 * width:3 * width]
        gate = x_ref[rows, 3 * width:4 * width]

        ez = jnp.exp(-jnp.abs(zf))
        rz = 1.0 / (1.0 + ez)
        pos = zf >= 0.0
        sig_p = jnp.where(pos, rz, ez * rz)
        sig_n = jnp.where(pos, ez * rz, rz)
        logf = jnp.log(lb_floor + one_m_lb * sig_p)
        k = one_m_lb * sig_n

        hi, mid, lo = _split3(logf)
        b = (jnp.dot(tri, hi, preferred_element_type=F32)
             + jnp.dot(tri, mid, preferred_element_type=F32)
             + jnp.dot(tri, lo, preferred_element_type=F32))
        ex_low = (jnp.dot(w_low, hi, preferred_element_type=F32)
                  + jnp.dot(w_low, mid, preferred_element_type=F32))

        att = [jnp.where(diag, jnp.sum((q * k)[:, h * dk:(h + 1) * dk], axis=-1, keepdims=True), 0.0)
               for h in range(n_heads)]
        for l in range(1, n_lev + 1):
            blk, half = 1 << l, 1 << (l - 1)
            if l <= HG_LOW_LEVELS:
                ex = ex_low[(l - 1) * c:l * c]
            else:
                pieces = []
                for s0 in range(0, c, blk):
                    bd = b[s0 + half - 1:s0 + half]
                    pieces.append(bd - b[s0:s0 + half])
                    pieces.append(b[s0 + half:s0 + blk] - bd)
                ex = jnp.concatenate(pieces, axis=0)
            wgt = jnp.exp(ex)
            ql, kl = (q * wgt).astype(BF16), (k * wgt).astype(BF16)
            for h in range(n_heads):
                sl = slice(h * dk, (h + 1) * dk)
                pm = lax.dot_general(ql[:, sl], kl[:, sl], nt, preferred_element_type=F32)
                att[h] = att[h] + jnp.where(keeps[l - 1], pm, 0.0)

        b_last = b[c - 1:c]
        q_in = (q * jnp.exp(b)).astype(BF16)
        k_out = (k * jnp.exp(b_last - b)).astype(BF16)
        e_last = jnp.exp(b_last)
        vb = v.astype(BF16)
        for h in range(n_heads):
            sl = slice(h * dk, (h + 1) * dk)
            st = st_ref[h]
            o = lax.dot_general(q_in[:, sl], st.astype(BF16), nt, preferred_element_type=F32)
            o = o + jnp.dot(att[h].astype(BF16), vb[:, sl], preferred_element_type=F32)
            st_ref[h] = st * e_last[:, sl] + lax.dot_general(vb[:, sl], k_out[:, sl], tn,
                                                              preferred_element_type=F32)
            o = o * lax.rsqrt(jnp.mean(o * o, axis=-1, keepdims=True) + EPS) * hnw[:, sl]
            gh = gate[:, sl]
            o_ref[rows, sl] = o * (gh * _sigmoid(gh))


def _hgrn_mix(x, lb_logits, head_norm, layer, n_batch):
    m = x.shape[0]
    n_layers, width = lb_logits.shape
    n_heads = head_norm.shape[0]
    dk = width // n_heads
    rows = HG_BLOCK
    per = m // n_batch // rows
    kern = functools.partial(_hgrn_kernel, layer=layer, n_heads=n_heads, chunk=HG_CHUNK)
    return pl.pallas_call(
        kern,
        grid=(n_batch, per),
        in_specs=[pl.BlockSpec((rows, 4 * width), lambda b, i: (b * per + i, 0)),
                  _const_spec((n_layers, width)),
                  _const_spec((1, width))],
        out_specs=pl.BlockSpec((rows, width), lambda b, i: (b * per + i, 0)),
        out_shape=jax.ShapeDtypeStruct((m, width), F32),
        scratch_shapes=[pltpu.VMEM((n_heads, dk, dk), F32)],
        compiler_params=_cparams("parallel", "arbitrary"),
        name="hgrn_mix",
    )(x, lb_logits, head_norm.reshape(1, width))


_STACK = (0, 2, 1, 3)


def _attn_kernel(sink_ref, q_ref, kv_ref, kvp_ref, o_ref, bias_ref, sinkc_ref, *, n_q, n_kv):
    tq = q_ref.shape[0]
    hd = q_ref.shape[1] // n_q
    grp = n_q // n_kv
    w = WINDOW
    lanes = 2 * hd
    nkw = n_kv * lanes
    first_block = pl.program_id(1) == 0

    @pl.when((pl.program_id(0) == 0) & first_block)
    def _():
        rr = lax.broadcasted_iota(jnp.int32, (2 * w, grp * w), 0)
        cc = lax.broadcasted_iota(jnp.int32, (2 * w, grp * w), 1)
        dist = cc % w + w - rr
        valid = (dist >= 0) & (dist < w)
        distf = dist.astype(F32)
        pos = cc // w
        posr = lax.broadcasted_iota(jnp.int32, (8, grp * w), 1) // w
        for kh in range(n_kv):
            slope = jnp.zeros((2 * w, grp * w), F32)
            sink = jnp.zeros((8, grp * w), F32)
            for i, g in enumerate(_STACK):
                head = kh * grp + g
                slope = jnp.where(pos == i, _LOG2E * 2.0 ** (-8.0 * (head + 1) / n_q), slope)
                sink = jnp.where(posr == i, _LOG2E * sink_ref[head], sink)
            b = jnp.where(valid, -slope * distf, NEG_BIG)
            bias_ref[0, kh] = b
            bias_ref[1, kh] = jnp.where(rr >= w, b, NEG_BIG)
            sinkc_ref[kh] = sink

    lane_q = lax.broadcasted_iota(jnp.int32, (w, lanes), 1)
    lo_q = lane_q < hd
    lane_v = lax.broadcasted_iota(jnp.int32, (2 * w, lanes), 1)
    lo_v = lane_v < hd
    zq = jnp.zeros((w, lanes), BF16)
    zv = jnp.zeros((2 * w, lanes), BF16)
    first_idx = jnp.where(first_block, 1, 0)

    nt = (((1,), (1,)), ((), ()))
    tn = (((0,), (0,)), ((), ()))

    def scores(jb, kh):
        rs = slice(jb * w, (jb + 1) * w)
        ks = slice(kh * lanes, (kh + 1) * lanes)
        if jb == 0:
            kprev = kvp_ref[:, ks]
            bias = bias_ref[first_idx, kh]
        else:
            kprev = kv_ref[(jb - 1) * w:jb * w, ks]
            bias = bias_ref[0, kh]
        kw = jnp.concatenate([kprev, kv_ref[rs, ks]], axis=0)
        qa = q_ref[rs, (2 * kh) * lanes:(2 * kh + 1) * lanes]
        qb = q_ref[rs, (2 * kh + 1) * lanes:(2 * kh + 2) * lanes]
        qs = jnp.concatenate([jnp.where(lo_q, qa, zq), jnp.where(lo_q, qb, zq),
                              jnp.where(lo_q, zq, qa), jnp.where(lo_q, zq, qb)], axis=0)
        return lax.dot_general(kw, qs, nt, preferred_element_type=F32) + bias

    def finish(jb, kh, st):
        rs = slice(jb * w, (jb + 1) * w)
        vs = slice(nkw + kh * lanes, nkw + (kh + 1) * lanes)
        vprev = kvp_ref[:, vs] if jb == 0 else kv_ref[(jb - 1) * w:jb * w, vs]
        vw = jnp.concatenate([vprev, kv_ref[rs, vs]], axis=0)
        sink = sinkc_ref[kh][0:1]
        mx = jnp.maximum(jnp.max(st, axis=0, keepdims=True), sink)
        pe = jnp.exp2(st - mx)
        den = jnp.sum(pe, axis=0, keepdims=True) + jnp.exp2(sink - mx)
        pn = (pe * (1.0 / den)).astype(BF16)
        o = (lax.dot_general(pn[:, 0:2 * w], jnp.where(lo_v, vw, zv), tn, preferred_element_type=F32)
             + lax.dot_general(pn[:, 2 * w:4 * w], jnp.where(lo_v, zv, vw), tn, preferred_element_type=F32))
        o_ref[rs, (2 * kh) * lanes:(2 * kh + 1) * lanes] = o[0:w].astype(o_ref.dtype)
        o_ref[rs, (2 * kh + 1) * lanes:(2 * kh + 2) * lanes] = o[w:2 * w].astype(o_ref.dtype)

    items = [(jb, kh) for jb in range(tq // w) for kh in range(n_kv)]
    pending = [scores(*items[i]) for i in range(ATTN_AHEAD)]
    for idx, item in enumerate(items):
        st = pending.pop(0)
        if idx + ATTN_AHEAD < len(items):
            pending.append(scores(*items[idx + ATTN_AHEAD]))
        finish(*item, st)


def _attention(q, kv, sinks, n_batch):
    m, qw = q.shape
    n_q = sinks.shape[0]
    hd = qw // n_q
    n_kv = kv.shape[1] // (4 * hd)
    grp = n_q // n_kv
    assert grp == 4 and 2 * hd == 128
    tq = ATTN_BLOCK_Q
    per = m // n_batch // tq
    ratio = tq // WINDOW
    kern = functools.partial(_attn_kernel, n_q=n_q, n_kv=n_kv)
    return pl.pallas_call(
        kern,
        grid=(n_batch, per),
        in_specs=[pl.BlockSpec(memory_space=pltpu.SMEM),
                  pl.BlockSpec((tq, qw), lambda b, i: (b * per + i, 0)),
                  pl.BlockSpec((tq, kv.shape[1]), lambda b, i: (b * per + i, 0)),
                  pl.BlockSpec((WINDOW, kv.shape[1]),
                               lambda b, i: (jnp.maximum((b * per + i) * ratio - 1, 0), 0))],
        out_specs=pl.BlockSpec((tq, qw), lambda b, i: (b * per + i, 0)),
        out_shape=jax.ShapeDtypeStruct((m, qw), BF16),
        scratch_shapes=[pltpu.VMEM((2, n_kv, 2 * WINDOW, grp * WINDOW), F32),
                        pltpu.VMEM((n_kv, 8, grp * WINDOW), F32)],
        compiler_params=_cparams("arbitrary", "arbitrary"),
        name="swa_attention",
    )(sinks, q, kv, kv)


def _dup_heads(wcols, n_heads):
    d = wcols.shape[0]
    wh = wcols.reshape(d, n_heads, -1)
    return jnp.concatenate([wh, wh], axis=-1).reshape(d, -1)


def _tail_kernel(*refs, even, final, ff_block):
    refs = list(refs)
    n_out = len(refs) - 1 if even else len(refs)
    o_ref = refs.pop(n_out - 1)
    h_ref = refs.pop(0)
    if even:
        ya_ref, yb_ref, wglu_ref, bglu_ref = refs[:4]
        refs = refs[4:]
    else:
        a_ref = refs.pop(0)
    p_ref, wout_ref, g2_ref, w1_ref, w2_ref, g3_ref, wup_ref, wgate_ref = refs[:8]
    gf_ref = refs[8] if final else None

    h = h_ref[...]
    if even:
        z = jax.nn.gelu(_from_chunk_rows(ya_ref, refs[-1], S5_CHUNK))
        oa = z * _sigmoid(_bdot(z, wglu_ref[...]) + bglu_ref[...])
        wa = oa.shape[-1]
        mix = (_bdot(oa, wout_ref[0:wa, :]) + _bdot(yb_ref[...], wout_ref[wa:, :]))
    else:
        mix = _bdot(a_ref[...], wout_ref[...])
    h = h + mix

    hn = _rms(h, g2_ref[...]).astype(BF16)
    d_ff = w1_ref.shape[1]
    acc = jnp.zeros_like(h)
    for c0 in range(0, d_ff, ff_block):
        a = jnp.dot(hn, w1_ref[:, c0:c0 + ff_block], preferred_element_type=F32)
        a = jnp.square(jnp.maximum(a, 0.0)).astype(BF16)
        acc = acc + jnp.dot(a, w2_ref[c0:c0 + ff_block, :], preferred_element_type=F32)
    h = h + acc

    hn = _rms(h, g3_ref[...])
    gate = _sigmoid(_bdot(hn, wgate_ref[...]))
    h = h + _bdot(p_ref[...], wup_ref[...]) * gate
    if final:
        h = _rms(h, gf_ref[...])
    o_ref[...] = h


def _tail(h, mixer_ins, mixer_weights, p, w_out, g2, w1, w2, g3, w_up, w_gate, g_final, even):
    m, d = h.shape
    tm = TOKEN_BLOCK
    tok = lambda a: pl.BlockSpec((tm, a.shape[1]), lambda i: (i, 0))
    rowv = lambda a: a.reshape(1, -1)
    weights = [w_out, rowv(g2), w1, w2, rowv(g3), w_up, w_gate]
    final = g_final is not None
    if final:
        weights.append(rowv(g_final))
    args = [h, *mixer_ins, *mixer_weights, p, *weights]
    specs = ([tok(h)] + [tok(a) for a in mixer_ins] + [_const_spec(a.shape) for a in mixer_weights]
             + [tok(p)] + [_const_spec(a.shape) for a in weights])
    scratch = []
    if even:
        ya = mixer_ins[0]
        specs[1] = pl.BlockSpec((tm // S5_CHUNK, ya.shape[1]), lambda i: (i, 0))
        scratch = [pltpu.VMEM((ya.shape[1] // S5_CHUNK // LANES, tm, LANES), F32)]
    kern = functools.partial(_tail_kernel, even=even, final=final, ff_block=1024)
    return pl.pallas_call(
        kern,
        grid=(m // tm,),
        in_specs=specs,
        out_specs=pl.BlockSpec((tm, d), lambda i: (i, 0)),
        out_shape=jax.ShapeDtypeStruct((m, d), F32),
        scratch_shapes=scratch,
        compiler_params=_cparams("parallel"),
        name="layer_tail",
    )(*args)


def kernel(x, p, mix_norm, mlp_norm, ple_norm, final_norm, w_in_even, w_out_even, s5_lam_re, s5_lam_im, s5_log_dt, s5_b_re, s5_b_im, s5_c_re, s5_c_im, s5_d, s5_w_glu, s5_b_glu, hgrn_lb_logits, hgrn_norm, w_qkv_odd, w_o_odd, attn_sinks, w_mlp_in, w_mlp_out, w_ple_up, w_ple_gate):
    bsz, seq, d = x.shape
    depth = p.shape[0]
    m = bsz * seq
    s5_w = s5_w_glu.shape[-1]
    hg_w = hgrn_lb_logits.shape[-1]
    n_q = attn_sinks.shape[-1]
    hd = d // n_q
    kv_w = w_qkv_odd.shape[-1] - n_q * hd
    assert seq % max(TOKEN_BLOCK, ATTN_BLOCK_Q, HG_BLOCK, S5_CHUNK) == 0

    h = x.reshape(m, d)
    for i in range(depth):
        j = i // 2
        if i % 2 == 0:
            u, hg = _norm_proj(h, mix_norm[i], w_in_even[j].astype(BF16),
                               [(s5_w, F32, 1.0), (4 * hg_w, F32, 1.0)], chunk_rows=S5_CHUNK)
            ya = _s5_mix(u, s5_lam_re[j], s5_lam_im[j], s5_log_dt[j], s5_b_re[j], s5_b_im[j],
                         s5_c_re[j], s5_c_im[j], s5_d[j], bsz)
            yb = _hgrn_mix(hg, hgrn_lb_logits, hgrn_norm[j], j, bsz)
            mixer_ins = [ya, yb]
            mixer_weights = [s5_w_glu[j].astype(BF16), s5_b_glu[j].reshape(1, -1)]
            w_out = w_out_even[j]
        else:
            wq, wk, wv = jnp.split(w_qkv_odd[j], [n_q * hd, n_q * hd + kv_w // 2], axis=-1)
            n_kv = kv_w // (2 * hd)
            w_qkv = jnp.concatenate([wq, _dup_heads(wk, n_kv), _dup_heads(wv, n_kv)], axis=-1)
            q, kv = _norm_proj(h, mix_norm[i], w_qkv.astype(BF16),
                               [(n_q * hd, BF16, _LOG2E / math.sqrt(hd)), (2 * kv_w, BF16, 1.0)])
            mixer_ins = [_attention(q, kv, attn_sinks[j], bsz)]
            mixer_weights = []
            w_out = w_o_odd[j]
        h = _tail(h, mixer_ins, mixer_weights, p[i].reshape(m, -1), w_out.astype(BF16),
                  mlp_norm[i], w_mlp_in[i].astype(BF16), w_mlp_out[i].astype(BF16),
                  ple_norm[i], w_ple_up[i].astype(BF16), w_ple_gate[i].astype(BF16),
                  final_norm if i == depth - 1 else None, even=(i % 2 == 0))
    return h.reshape(bsz, seq, d)
```

```python
import functools
import math

import jax
import jax.numpy as jnp
from jax import lax
from jax.experimental import pallas as pl
from jax.experimental.pallas import tpu as pltpu

F32 = jnp.float32
BF16 = jnp.bfloat16

EPS = 1e-6
WINDOW = 128
S5_CHUNK = 16
LANES = 128
HG_CHUNK = 128
HG_BLOCK = 512
LAM_RE_MAX = -1e-4
NEG_BIG = -1e30
_LOG2E = math.log2(math.e)

_V7X_VMEM_BYTES = 64 * 1024 * 1024
_VMEM_LIMIT = _V7X_VMEM_BYTES - 12 * 1024 * 1024

TOKEN_BLOCK = 512
ATTN_BLOCK_Q = 512
ATTN_AHEAD = 2


def _cparams(*sem):
    return pltpu.CompilerParams(dimension_semantics=sem, vmem_limit_bytes=_VMEM_LIMIT)


def _const_spec(shape):
    nd = len(shape)
    return pl.BlockSpec(shape, lambda *_: (0,) * nd, pipeline_mode=pl.Buffered(1))


def _rms(x, g):
    ms = jnp.mean(x * x, axis=-1, keepdims=True)
    return x * lax.rsqrt(ms + EPS) * g


def _sigmoid(x):
    return 1.0 / (1.0 + jnp.exp(-x))


def _bdot(a, b):
    return jnp.dot(a.astype(BF16), b.astype(BF16), preferred_element_type=F32)


def _to_chunk_rows(val, o_ref, scr_ref, chunk):
    n_tiles = scr_ref.shape[0]
    rows = o_ref.shape[0]
    for k in range(n_tiles):
        scr_ref[k] = val[:, k * LANES:(k + 1) * LANES]
    for k in range(n_tiles):
        for t in range(chunk):
            o_ref[:, (k * chunk + t) * LANES:(k * chunk + t + 1) * LANES] = scr_ref[k, pl.ds(t, rows, stride=chunk), :]


def _from_chunk_rows(x_ref, scr_ref, chunk):
    n_tiles = scr_ref.shape[0]
    rows = x_ref.shape[0]
    for k in range(n_tiles):
        for t in range(chunk):
            scr_ref[k, pl.ds(t, rows, stride=chunk), :] = x_ref[:, (k * chunk + t) * LANES:(k * chunk + t + 1) * LANES]
    return jnp.concatenate([scr_ref[k] for k in range(n_tiles)], axis=1)


def _proj_kernel(h_ref, g_ref, w_ref, *refs, scales, widths, chunk_rows):
    out_refs = refs[:len(scales)]
    hn = _rms(h_ref[...], g_ref[...]).astype(BF16)
    off = 0
    for i, (o_ref, s, n) in enumerate(zip(out_refs, scales, widths)):
        acc = jnp.dot(hn, w_ref[:, off:off + n], preferred_element_type=F32)
        if s != 1.0:
            acc = acc * s
        if i == 0 and chunk_rows:
            _to_chunk_rows(acc, o_ref, refs[-1], chunk_rows)
        else:
            o_ref[...] = acc.astype(o_ref.dtype)
        off += n


def _norm_proj(h, g, w, outs, chunk_rows=0):
    m, d = h.shape
    tm = TOKEN_BLOCK
    widths = tuple(n for n, _, _ in outs)
    kern = functools.partial(_proj_kernel, scales=tuple(s for _, _, s in outs), widths=widths,
                             chunk_rows=chunk_rows)
    out_specs = [pl.BlockSpec((tm, n), lambda i: (i, 0)) for n in widths]
    out_shape = [jax.ShapeDtypeStruct((m, n), dt) for n, dt, _ in outs]
    scratch = []
    if chunk_rows:
        n0 = widths[0]
        out_specs[0] = pl.BlockSpec((tm // chunk_rows, n0 * chunk_rows), lambda i: (i, 0))
        out_shape[0] = jax.ShapeDtypeStruct((m // chunk_rows, n0 * chunk_rows), outs[0][1])
        scratch = [pltpu.VMEM((n0 // LANES, tm, LANES), F32)]
    return pl.pallas_call(
        kern,
        grid=(m // tm,),
        in_specs=[pl.BlockSpec((tm, d), lambda i: (i, 0)),
                  _const_spec((1, d)),
                  _const_spec(w.shape)],
        out_specs=out_specs,
        out_shape=out_shape,
        scratch_shapes=scratch,
        compiler_params=_cparams("parallel"),
        name="norm_proj",
    )(h, g.reshape(1, d), w)


def _cmul(ar, ai, br, bi):
    return ar * br - ai * bi, ar * bi + ai * br


def _s5_discretize(lam_re, lam_im, dt):
    lr = jnp.minimum(lam_re, LAM_RE_MAX)
    li = lam_im
    mag = jnp.exp(lr * dt)
    ar = mag * jnp.cos(li * dt)
    ai = mag * jnp.sin(li * dt)
    den = lr * lr + li * li
    xr = ar - 1.0
    zr = (xr * lr + ai * li) / den
    zi = (ai * lr - xr * li) / den
    return ar, ai, zr, zi


def _s5_kernel(u_ref, lam_re_ref, lam_im_ref, logdt_ref, ct_re_ref, ct_im_ref, b_re_ref, b_im_ref,
               lamr_re_ref, lamr_im_ref, logdtr_ref, bt_re_ref, bt_im_ref, d_ref,
               lams_re_ref, lams_im_ref, logdts_ref,
               y_ref, m_ref, qt_ref, pm_ref, *, chunk, n_chan):
    t, hh = chunk, n_chan
    tw = t * LANES
    p = lam_re_ref.shape[1]
    gt = LANES // hh
    sw = gt * p
    nc = u_ref.shape[0]
    n_cbits = max(1, (nc - 1).bit_length())

    @pl.when(pl.program_id(1) == 0)
    def _():
        dt = jnp.exp(logdt_ref[0])
        ar, ai, zr, zi = _s5_discretize(lam_re_ref[0], lam_im_ref[0], dt)
        pw = [(jnp.ones((p, LANES), F32), jnp.zeros((p, LANES), F32))]
        for _ in range(t):
            pw.append(_cmul(pw[-1][0], pw[-1][1], ar, ai))
        bbr, bbi = _cmul(zr, zi, b_re_ref[0], b_im_ref[0])
        ctr, cti = ct_re_ref[0], ct_im_ref[0]
        lane_g = lax.broadcasted_iota(jnp.int32, (p, LANES), 1) // hh
        zero = jnp.zeros((p, LANES), F32)
        q_slots = [_cmul(pw[t - 1 - s][0], pw[t - 1 - s][1], bbr, bbi) for s in range(t)]
        c_slots = [_cmul(ctr, cti, pw[s + 1][0], pw[s + 1][1]) for s in range(t)]
        for g2 in range(gt):
            own = lane_g == g2
            rows_re = slice(g2 * p, (g2 + 1) * p)
            rows_im = slice(sw + g2 * p, sw + (g2 + 1) * p)
            qt_ref[rows_re, :] = jnp.concatenate([jnp.where(own, r, zero) for r, _ in q_slots], axis=1).astype(BF16)
            qt_ref[rows_im, :] = jnp.concatenate([jnp.where(own, i, zero) for _, i in q_slots], axis=1).astype(BF16)
            pm_ref[rows_re, :] = jnp.concatenate([jnp.where(own, r, zero) for r, _ in c_slots], axis=1).astype(BF16)
            pm_ref[rows_im, :] = jnp.concatenate([jnp.where(own, -i, zero) for _, i in c_slots], axis=1).astype(BF16)

        _, _, zr2, zi2 = _s5_discretize(lamr_re_ref[0], lamr_im_ref[0], jnp.exp(logdtr_ref[0]))
        first = lax.broadcasted_iota(jnp.int32, (LANES, 2 * p), 1) < p
        btr, bti = bt_re_ref[0], bt_im_ref[0]
        lhs = jnp.where(first, zr2 * btr - zi2 * bti, -(zr2 * bti + zi2 * btr))
        rhs = jnp.concatenate(
            [jnp.concatenate([_cmul(ctr, cti, pw[tau][0], pw[tau][1])[0] for tau in range(t)], axis=1),
             jnp.concatenate([_cmul(ctr, cti, pw[tau][0], pw[tau][1])[1] for tau in range(t)], axis=1)],
            axis=0)
        lh, ll = lhs.astype(BF16), (lhs - lhs.astype(BF16).astype(F32)).astype(BF16)
        rh, rl = rhs.astype(BF16), (rhs - rhs.astype(BF16).astype(F32)).astype(BF16)
        z = (jnp.dot(lh, rh, preferred_element_type=F32)
             + jnp.dot(lh, rl, preferred_element_type=F32)
             + jnp.dot(ll, rh, preferred_element_type=F32))
        row_z = lax.broadcasted_iota(jnp.int32, (LANES, tw), 0)
        lane_z = lax.broadcasted_iota(jnp.int32, (LANES, tw), 1)
        z = jnp.where(row_z // hh == (lane_z % LANES) // hh, z, 0.0)
        d_diag = jnp.where(lax.broadcasted_iota(jnp.int32, (LANES, LANES), 0)
                           == lax.broadcasted_iota(jnp.int32, (LANES, LANES), 1), d_ref[0], 0.0)
        z = jnp.concatenate([z[:, 0:LANES] + d_diag, z[:, LANES:]], axis=1)
        zb = z.astype(BF16)
        for s_in in range(t):
            if s_in:
                m_ref[s_in * LANES:(s_in + 1) * LANES, 0:s_in * LANES] = jnp.zeros((LANES, s_in * LANES), BF16)
            m_ref[s_in * LANES:(s_in + 1) * LANES, s_in * LANES:tw] = zb[:, 0:tw - s_in * LANES]

    u = u_ref[...].astype(BF16)
    y = jnp.concatenate(
        [jnp.dot(u[:, 0:(i + 2) * LANES], m_ref[0:(i + 2) * LANES, i * LANES:(i + 2) * LANES],
                 preferred_element_type=F32) for i in range(0, t, 2)], axis=1)
    x = lax.dot_general(u, qt_ref[...], (((1,), (1,)), ((), ())), preferred_element_type=F32)

    mr, mi, _, _ = _s5_discretize(lams_re_ref[0], lams_im_ref[0], jnp.exp(logdts_ref[0]))
    for _ in range(t.bit_length() - 1):
        mr, mi = _cmul(mr, mi, mr, mi)
    ridx = lax.broadcasted_iota(jnp.int32, (nc, sw), 0)

    def shift_down(v, k):
        if k % 8 == 0:
            return jnp.concatenate([jnp.zeros((k, sw), F32), v[:nc - k]], axis=0)
        return jnp.where(ridx >= k, pltpu.roll(v, k, axis=0), 0.0)

    s_re, s_im = x[:, 0:sw], x[:, sw:2 * sw]
    for k in range(n_cbits):
        h_re, h_im = shift_down(s_re, 1 << k), shift_down(s_im, 1 << k)
        s_re, s_im = s_re + (mr * h_re - mi * h_im), s_im + (mr * h_im + mi * h_re)
        mr, mi = _cmul(mr, mi, mr, mi)
    s_prev = jnp.concatenate([shift_down(s_re, 1), shift_down(s_im, 1)], axis=1).astype(BF16)
    y_ref[...] = y + jnp.dot(s_prev, pm_ref[...], preferred_element_type=F32)


def _s5_mix(u, lam_re, lam_im, log_dt, b_re, b_im, c_re, c_im, d_skip, n_batch):
    rows, _ = u.shape
    g, p, hh = b_re.shape
    t = S5_CHUNK
    gt = LANES // hh
    nt = g // gt
    tw = t * LANES
    nc = rows // n_batch

    def lanes_gh(a):
        return a.reshape(nt, gt, p, hh).transpose(0, 2, 1, 3).reshape(nt, p, LANES)

    def rows_gh(a):
        r = a.reshape(nt, gt, p, hh).transpose(0, 1, 3, 2).reshape(nt, LANES, p)
        return jnp.concatenate([r, r], axis=-1)

    per_gh = lambda a: jnp.broadcast_to(a.reshape(g, p, 1), (g, p, hh))
    lam_re_c, lam_im_c = lanes_gh(per_gh(lam_re)), lanes_gh(per_gh(lam_im))
    logdt_c = jnp.broadcast_to(log_dt.reshape(nt, gt, 1), (nt, gt, hh)).reshape(nt, 1, LANES)
    ct_re, ct_im = lanes_gh(c_re.transpose(0, 2, 1)), lanes_gh(c_im.transpose(0, 2, 1))
    b_re_c, b_im_c = lanes_gh(b_re), lanes_gh(b_im)
    lamr = lambda a: rows_gh(jnp.broadcast_to(a.reshape(g, p, 1), (g, p, hh)))
    logdt_r = jnp.broadcast_to(log_dt.reshape(nt, gt, 1), (nt, gt, hh)).reshape(nt, LANES, 1)
    d_c = d_skip.reshape(nt, 1, LANES)
    lams = lambda a: a.reshape(nt, 1, gt * p)
    logdt_s = jnp.broadcast_to(log_dt.reshape(nt, gt, 1), (nt, gt, p)).reshape(nt, 1, gt * p)

    params = [lam_re_c, lam_im_c, logdt_c, ct_re, ct_im, b_re_c, b_im_c,
              lamr(lam_re), lamr(lam_im), logdt_r, rows_gh(b_re), rows_gh(b_im), d_c,
              lams(lam_re), lams(lam_im), logdt_s]
    pspec = lambda a: pl.BlockSpec((1,) + a.shape[1:], lambda k, b: (k,) + (0,) * (a.ndim - 1))
    kern = functools.partial(_s5_kernel, chunk=t, n_chan=hh)
    return pl.pallas_call(
        kern,
        grid=(nt, n_batch),
        in_specs=[pl.BlockSpec((nc, tw), lambda k, b: (b, k))] + [pspec(a) for a in params],
        out_specs=pl.BlockSpec((nc, tw), lambda k, b: (b, k)),
        out_shape=jax.ShapeDtypeStruct(u.shape, F32),
        scratch_shapes=[pltpu.VMEM((tw, tw), BF16),
                        pltpu.VMEM((2 * gt * p, tw), BF16),
                        pltpu.VMEM((2 * gt * p, tw), BF16)],
        compiler_params=_cparams("arbitrary", "arbitrary"),
        name="s5_mix",
    )(u, *params)


HG_LOW_LEVELS = 3


def _hgrn_kernel(x_ref, lbl_ref, hn_ref, o_ref, st_ref, *, layer, n_heads, chunk):
    c = chunk
    n_sub = x_ref.shape[0] // c
    width = o_ref.shape[-1]
    dk = width // n_heads
    n_lev = c.bit_length() - 1

    @pl.when(pl.program_id(1) == 0)
    def _():
        st_ref[...] = jnp.zeros_like(st_ref)

    lg = lbl_ref[...]
    e = jnp.exp(lg - jnp.max(lg, axis=0, keepdims=True))
    sm = e / jnp.sum(e, axis=0, keepdims=True)
    lb = jnp.zeros((1, width), F32)
    for r in range(1, layer + 1):
        lb = lb + sm[r:r + 1]
    lb_floor = jnp.maximum(lb, 1e-30)
    one_m_lb = 1.0 - lb
    hnw = hn_ref[...]

    ri = lax.broadcasted_iota(jnp.int32, (c, c), 0)
    ci = lax.broadcasted_iota(jnp.int32, (c, c), 1)
    tri = jnp.where(ci <= ri, 1.0, 0.0).astype(BF16)
    keeps, lows = [], []
    for l in range(1, n_lev + 1):
        blk, half = 1 << l, 1 << (l - 1)
        keeps.append((ri // blk == ci // blk) & (ri % blk >= half) & (ci % blk < half))
        if l <= HG_LOW_LEVELS:
            bd = (ri // blk) * blk + half - 1
            lows.append(jnp.where(((ci > bd) & (ci <= ri)) | ((ci > ri) & (ci <= bd)), 1.0, 0.0))
    w_low = jnp.concatenate(lows, axis=0).astype(BF16)
    diag = ri == ci

    nt = (((1,), (1,)), ((), ()))
    tn = (((0,), (0,)), ((), ()))
    def decays(sub):
        rows = slice(sub * c, (sub + 1) * c)
        q = x_ref[rows, 0:width]
        zf = x_ref[rows, width:2 * width]
        sig = 1.0 / (1.0 + jnp.exp(-zf))
        logf = jnp.log(lb_floor + one_m_lb * sig)
        k = one_m_lb * (1.0 - sig)
        hi = logf.astype(BF16)
        mid = (logf - hi.astype(F32)).astype(BF16)
        b = jnp.dot(tri, hi, preferred_element_type=F32) + jnp.dot(tri, mid, preferred_element_type=F32)
        ex_low = (jnp.dot(w_low, hi, preferred_element_type=F32)
                  + jnp.dot(w_low, mid, preferred_element_type=F32))
        return q, k, b, ex_low

    def mix(sub, q, k, b, ex_low):
        rows = slice(sub * c, (sub + 1) * c)
        v = x_ref[rows, 2 * width:3 * width]
        gate = x_ref[rows, 3 * width:4 * width]
        qb, kb = q.astype(BF16), k.astype(BF16)
        att = [jnp.where(diag, jnp.sum((q * k)[:, h * dk:(h + 1) * dk], axis=-1, keepdims=True), 0.0)
               for h in range(n_heads)]
        for l in range(1, n_lev + 1):
            blk, half = 1 << l, 1 << (l - 1)
            if l <= HG_LOW_LEVELS:
                ex = ex_low[(l - 1) * c:l * c]
            else:
                pieces = []
                for s0 in range(0, c, blk):
                    bd = b[s0 + half - 1:s0 + half]
                    pieces.append(bd - b[s0:s0 + half])
                    pieces.append(b[s0 + half:s0 + blk] - bd)
                ex = jnp.concatenate(pieces, axis=0)
            wgt = jnp.exp(ex).astype(BF16)
            ql, kl = qb * wgt, kb * wgt
            for h in range(n_heads):
                sl = slice(h * dk, (h + 1) * dk)
                pm = lax.dot_general(ql[:, sl], kl[:, sl], nt, preferred_element_type=F32)
                att[h] = att[h] + jnp.where(keeps[l - 1], pm, 0.0)

        b_last = b[c - 1:c]
        q_in = (q * jnp.exp(b)).astype(BF16)
        k_out = (k * jnp.exp(b_last - b)).astype(BF16)
        e_last = jnp.exp(b_last)
        vb = v.astype(BF16)
        for h in range(n_heads):
            sl = slice(h * dk, (h + 1) * dk)
            st = st_ref[h]
            o = lax.dot_general(q_in[:, sl], st.astype(BF16), nt, preferred_element_type=F32)
            o = o + jnp.dot(att[h].astype(BF16), vb[:, sl], preferred_element_type=F32)
            st_ref[h] = st * e_last[:, sl] + lax.dot_general(vb[:, sl], k_out[:, sl], tn,
                                                              preferred_element_type=F32)
            o = o * lax.rsqrt(jnp.mean(o * o, axis=-1, keepdims=True) + EPS) * hnw[:, sl]
            gh = gate[:, sl]
            o_ref[rows, sl] = o * (gh * _sigmoid(gh))

    nxt = decays(0)
    for sub in range(n_sub):
        cur = nxt
        if sub + 1 < n_sub:
            nxt = decays(sub + 1)
        mix(sub, *cur)


def _hgrn_mix(x, lb_logits, head_norm, layer, n_batch):
    m = x.shape[0]
    n_layers, width = lb_logits.shape
    n_heads = head_norm.shape[0]
    dk = width // n_heads
    rows = HG_BLOCK
    per = m // n_batch // rows
    kern = functools.partial(_hgrn_kernel, layer=layer, n_heads=n_heads, chunk=HG_CHUNK)
    return pl.pallas_call(
        kern,
        grid=(n_batch, per),
        in_specs=[pl.BlockSpec((rows, 4 * width), lambda b, i: (b * per + i, 0)),
                  _const_spec((n_layers, width)),
                  _const_spec((1, width))],
        out_specs=pl.BlockSpec((rows, width), lambda b, i: (b * per + i, 0)),
        out_shape=jax.ShapeDtypeStruct((m, width), F32),
        scratch_shapes=[pltpu.VMEM((n_heads, dk, dk), F32)],
        compiler_params=_cparams("parallel", "arbitrary"),
        name="hgrn_mix",
    )(x, lb_logits, head_norm.reshape(1, width))


_STACK = (0, 2, 1, 3)


def _attn_kernel(sink_ref, q_ref, kv_ref, kvp_ref, o_ref, bias_ref, sinkc_ref, *, n_q, n_kv):
    tq = q_ref.shape[0]
    hd = q_ref.shape[1] // n_q
    grp = n_q // n_kv
    w = WINDOW
    lanes = 2 * hd
    nkw = n_kv * lanes
    first_block = pl.program_id(1) == 0

    @pl.when((pl.program_id(0) == 0) & first_block)
    def _():
        rr = lax.broadcasted_iota(jnp.int32, (2 * w, grp * w), 0)
        cc = lax.broadcasted_iota(jnp.int32, (2 * w, grp * w), 1)
        dist = cc % w + w - rr
        valid = (dist >= 0) & (dist < w)
        distf = dist.astype(F32)
        pos = cc // w
        posr = lax.broadcasted_iota(jnp.int32, (8, grp * w), 1) // w
        for kh in range(n_kv):
            slope = jnp.zeros((2 * w, grp * w), F32)
            sink = jnp.zeros((8, grp * w), F32)
            for i, g in enumerate(_STACK):
                head = kh * grp + g
                slope = jnp.where(pos == i, _LOG2E * 2.0 ** (-8.0 * (head + 1) / n_q), slope)
                sink = jnp.where(posr == i, _LOG2E * sink_ref[head], sink)
            b = jnp.where(valid, -slope * distf, NEG_BIG)
            bias_ref[0, kh] = b
            bias_ref[1, kh] = jnp.where(rr >= w, b, NEG_BIG)
            sinkc_ref[kh] = sink

    lane_q = lax.broadcasted_iota(jnp.int32, (w, lanes), 1)
    lo_q = lane_q < hd
    lane_v = lax.broadcasted_iota(jnp.int32, (2 * w, lanes), 1)
    lo_v = lane_v < hd
    zq = jnp.zeros((w, lanes), BF16)
    zv = jnp.zeros((2 * w, lanes), BF16)
    first_idx = jnp.where(first_block, 1, 0)

    nt = (((1,), (1,)), ((), ()))
    tn = (((0,), (0,)), ((), ()))

    def scores(jb, kh):
        rs = slice(jb * w, (jb + 1) * w)
        ks = slice(kh * lanes, (kh + 1) * lanes)
        if jb == 0:
            kprev = kvp_ref[:, ks]
            bias = bias_ref[first_idx, kh]
        else:
            kprev = kv_ref[(jb - 1) * w:jb * w, ks]
            bias = bias_ref[0, kh]
        kw = jnp.concatenate([kprev, kv_ref[rs, ks]], axis=0)
        qa = q_ref[rs, (2 * kh) * lanes:(2 * kh + 1) * lanes]
        qb = q_ref[rs, (2 * kh + 1) * lanes:(2 * kh + 2) * lanes]
        qs = jnp.concatenate([jnp.where(lo_q, qa, zq), jnp.where(lo_q, qb, zq),
                              jnp.where(lo_q, zq, qa), jnp.where(lo_q, zq, qb)], axis=0)
        return lax.dot_general(kw, qs, nt, preferred_element_type=F32) + bias

    def finish(jb, kh, st):
        rs = slice(jb * w, (jb + 1) * w)
        vs = slice(nkw + kh * lanes, nkw + (kh + 1) * lanes)
        vprev = kvp_ref[:, vs] if jb == 0 else kv_ref[(jb - 1) * w:jb * w, vs]
        vw = jnp.concatenate([vprev, kv_ref[rs, vs]], axis=0)
        sink = sinkc_ref[kh][0:1]
        mx = jnp.maximum(jnp.max(st, axis=0, keepdims=True), sink)
        pe = jnp.exp2(st - mx)
        den = jnp.sum(pe, axis=0, keepdims=True) + jnp.exp2(sink - mx)
        pn = (pe * (1.0 / den)).astype(BF16)
        o = (lax.dot_general(pn[:, 0:2 * w], jnp.where(lo_v, vw, zv), tn, preferred_element_type=F32)
             + lax.dot_general(pn[:, 2 * w:4 * w], jnp.where(lo_v, zv, vw), tn, preferred_element_type=F32))
        o_ref[rs, (2 * kh) * lanes:(2 * kh + 1) * lanes] = o[0:w].astype(o_ref.dtype)
        o_ref[rs, (2 * kh + 1) * lanes:(2 * kh + 2) * lanes] = o[w:2 * w].astype(o_ref.dtype)

    items = [(jb, kh) for jb in range(tq // w) for kh in range(n_kv)]
    pending = [scores(*items[i]) for i in range(ATTN_AHEAD)]
    for idx, item in enumerate(items):
        st = pending.pop(0)
        if idx + ATTN_AHEAD < len(items):
            pending.append(scores(*items[idx + ATTN_AHEAD]))
        finish(*item, st)


def _attention(q, kv, sinks, n_batch):
    m, qw = q.shape
    n_q = sinks.shape[0]
    hd = qw // n_q
    n_kv = kv.shape[1] // (4 * hd)
    grp = n_q // n_kv
    assert grp == 4 and 2 * hd == 128
    tq = ATTN_BLOCK_Q
    per = m // n_batch // tq
    ratio = tq // WINDOW
    kern = functools.partial(_attn_kernel, n_q=n_q, n_kv=n_kv)
    return pl.pallas_call(
        kern,
        grid=(n_batch, per),
        in_specs=[pl.BlockSpec(memory_space=pltpu.SMEM),
                  pl.BlockSpec((tq, qw), lambda b, i: (b * per + i, 0)),
                  pl.BlockSpec((tq, kv.shape[1]), lambda b, i: (b * per + i, 0)),
                  pl.BlockSpec((WINDOW, kv.shape[1]),
                               lambda b, i: (jnp.maximum((b * per + i) * ratio - 1, 0), 0))],
        out_specs=pl.BlockSpec((tq, qw), lambda b, i: (b * per + i, 0)),
        out_shape=jax.ShapeDtypeStruct((m, qw), BF16),
        scratch_shapes=[pltpu.VMEM((2, n_kv, 2 * WINDOW, grp * WINDOW), F32),
                        pltpu.VMEM((n_kv, 8, grp * WINDOW), F32)],
        compiler_params=_cparams("arbitrary", "arbitrary"),
        name="swa_attention",
    )(sinks, q, kv, kv)


def _dup_heads(wcols, n_heads):
    d = wcols.shape[0]
    wh = wcols.reshape(d, n_heads, -1)
    return jnp.concatenate([wh, wh], axis=-1).reshape(d, -1)


def _tail_kernel(*refs, even, final, ff_block):
    refs = list(refs)
    n_out = len(refs) - 1 if even else len(refs)
    o_ref = refs.pop(n_out - 1)
    h_ref = refs.pop(0)
    if even:
        ya_ref, yb_ref, wglu_ref, bglu_ref = refs[:4]
        refs = refs[4:]
    else:
        a_ref = refs.pop(0)
    p_ref, wout_ref, g2_ref, w1_ref, w2_ref, g3_ref, wup_ref, wgate_ref = refs[:8]
    gf_ref = refs[8] if final else None

    h = h_ref[...]
    if even:
        z = jax.nn.gelu(_from_chunk_rows(ya_ref, refs[-1], S5_CHUNK))
        oa = z * _sigmoid(_bdot(z, wglu_ref[...]) + bglu_ref[...])
        wa = oa.shape[-1]
        mix = (_bdot(oa, wout_ref[0:wa, :]) + _bdot(yb_ref[...], wout_ref[wa:, :]))
    else:
        mix = _bdot(a_ref[...], wout_ref[...])
    h = h + mix

    hn = _rms(h, g2_ref[...]).astype(BF16)
    d_ff = w1_ref.shape[1]
    acc = jnp.zeros_like(h)
    for c0 in range(0, d_ff, ff_block):
        a = jnp.dot(hn, w1_ref[:, c0:c0 + ff_block], preferred_element_type=F32)
        a = jnp.square(jnp.maximum(a, 0.0)).astype(BF16)
        acc = acc + jnp.dot(a, w2_ref[c0:c0 + ff_block, :], preferred_element_type=F32)
    h = h + acc

    hn = _rms(h, g3_ref[...])
    gate = _sigmoid(_bdot(hn, wgate_ref[...]))
    h = h + _bdot(p_ref[...], wup_ref[...]) * gate
    if final:
        h = _rms(h, gf_ref[...])
    o_ref[...] = h


def _tail(h, mixer_ins, mixer_weights, p, layer, w_out, g2, w1, w2, g3, w_up, w_gate, g_final, even):
    m, d = h.shape
    tm = TOKEN_BLOCK
    tok = lambda a: pl.BlockSpec((tm, a.shape[1]), lambda i: (i, 0))
    rowv = lambda a: a.reshape(1, -1)
    weights = [w_out, rowv(g2), w1, w2, rowv(g3), w_up, w_gate]
    final = g_final is not None
    if final:
        weights.append(rowv(g_final))
    args = [h, *mixer_ins, *mixer_weights, p, *weights]
    specs = ([tok(h)] + [tok(a) for a in mixer_ins] + [_const_spec(a.shape) for a in mixer_weights]
             + [pl.BlockSpec((None, tm, p.shape[-1]), lambda i: (layer, i, 0))] + [_const_spec(a.shape) for a in weights])
    scratch = []
    if even:
        ya = mixer_ins[0]
        specs[1] = pl.BlockSpec((tm // S5_CHUNK, ya.shape[1]), lambda i: (i, 0))
        scratch = [pltpu.VMEM((ya.shape[1] // S5_CHUNK // LANES, tm, LANES), F32)]
    kern = functools.partial(_tail_kernel, even=even, final=final, ff_block=1024)
    return pl.pallas_call(
        kern,
        grid=(m // tm,),
        in_specs=specs,
        out_specs=pl.BlockSpec((tm, d), lambda i: (i, 0)),
        out_shape=jax.ShapeDtypeStruct((m, d), F32),
        scratch_shapes=scratch,
        compiler_params=_cparams("parallel"),
        name="layer_tail",
    )(*args)


def kernel(x, p, mix_norm, mlp_norm, ple_norm, final_norm, w_in_even, w_out_even, s5_lam_re, s5_lam_im, s5_log_dt, s5_b_re, s5_b_im, s5_c_re, s5_c_im, s5_d, s5_w_glu, s5_b_glu, hgrn_lb_logits, hgrn_norm, w_qkv_odd, w_o_odd, attn_sinks, w_mlp_in, w_mlp_out, w_ple_up, w_ple_gate):
    bsz, seq, d = x.shape
    depth = p.shape[0]
    m = bsz * seq
    s5_w = s5_w_glu.shape[-1]
    hg_w = hgrn_lb_logits.shape[-1]
    n_q = attn_sinks.shape[-1]
    hd = d // n_q
    kv_w = w_qkv_odd.shape[-1] - n_q * hd
    assert seq % max(TOKEN_BLOCK, ATTN_BLOCK_Q, HG_BLOCK, S5_CHUNK) == 0

    h = x.reshape(m, d)
    for i in range(depth):
        j = i // 2
        if i % 2 == 0:
            u, hg = _norm_proj(h, mix_norm[i], w_in_even[j].astype(BF16),
                               [(s5_w, F32, 1.0), (4 * hg_w, F32, 1.0)], chunk_rows=S5_CHUNK)
            ya = _s5_mix(u, s5_lam_re[j], s5_lam_im[j], s5_log_dt[j], s5_b_re[j], s5_b_im[j],
                         s5_c_re[j], s5_c_im[j], s5_d[j], bsz)
            yb = _hgrn_mix(hg, hgrn_lb_logits, hgrn_norm[j], j, bsz)
            mixer_ins = [ya, yb]
            mixer_weights = [s5_w_glu[j].astype(BF16), s5_b_glu[j].reshape(1, -1)]
            w_out = w_out_even[j]
        else:
            wq, wk, wv = jnp.split(w_qkv_odd[j], [n_q * hd, n_q * hd + kv_w // 2], axis=-1)
            n_kv = kv_w // (2 * hd)
            w_qkv = jnp.concatenate([wq, _dup_heads(wk, n_kv), _dup_heads(wv, n_kv)], axis=-1)
            q, kv = _norm_proj(h, mix_norm[i], w_qkv.astype(BF16),
                               [(n_q * hd, BF16, _LOG2E / math.sqrt(hd)), (2 * kv_w, BF16, 1.0)])
            mixer_ins = [_attention(q, kv, attn_sinks[j], bsz)]
            mixer_weights = []
            w_out = w_o_odd[j]
        h = _tail(h, mixer_ins, mixer_weights, p.reshape(depth, m, -1), i, w_out.astype(BF16),
                  mlp_norm[i], w_mlp_in[i].astype(BF16), w_mlp_out[i].astype(BF16),
                  ple_norm[i], w_ple_up[i].astype(BF16), w_ple_gate[i].astype(BF16),
                  final_norm if i == depth - 1 else None, even=(i % 2 == 0))
    return h.reshape(bsz, seq, d)
```

```python
import functools
import math

import jax
import jax.numpy as jnp
from jax import lax
from jax.experimental import pallas as pl
from jax.experimental.pallas import tpu as pltpu

F32 = jnp.float32
BF16 = jnp.bfloat16

EPS = 1e-6
WINDOW = 128
S5_CHUNK = 16
LANES = 128
HG_CHUNK = 128
HG_BLOCK = 512
LAM_RE_MAX = -1e-4
NEG_BIG = -1e30
_LOG2E = math.log2(math.e)

_V7X_VMEM_BYTES = 64 * 1024 * 1024
_VMEM_LIMIT = _V7X_VMEM_BYTES - 12 * 1024 * 1024

TOKEN_BLOCK = 512
TAIL_STREAMS = 2
PROJ_BLOCK = 1024
ATTN_BLOCK_Q = 512
ATTN_AHEAD = 2


def _cparams(*sem):
    return pltpu.CompilerParams(dimension_semantics=sem, vmem_limit_bytes=_VMEM_LIMIT)


def _const_spec(shape):
    nd = len(shape)
    return pl.BlockSpec(shape, lambda *_: (0,) * nd, pipeline_mode=pl.Buffered(1))


def _rms(x, g):
    ms = jnp.mean(x * x, axis=-1, keepdims=True)
    return x * lax.rsqrt(ms + EPS) * g


def _sigmoid(x):
    return 1.0 / (1.0 + jnp.exp(-x))


def _bdot(a, b):
    return jnp.dot(a.astype(BF16), b.astype(BF16), preferred_element_type=F32)


def _to_chunk_rows(val, o_ref, scr_ref, chunk):
    n_tiles = scr_ref.shape[0]
    rows = o_ref.shape[0]
    for k in range(n_tiles):
        scr_ref[k] = val[:, k * LANES:(k + 1) * LANES]
    for k in range(n_tiles):
        for t in range(chunk):
            o_ref[:, (k * chunk + t) * LANES:(k * chunk + t + 1) * LANES] = scr_ref[k, pl.ds(t, rows, stride=chunk), :]


def _from_chunk_rows(x_ref, scr_ref, chunk):
    n_tiles = scr_ref.shape[0]
    rows = x_ref.shape[0]
    for k in range(n_tiles):
        for t in range(chunk):
            scr_ref[k, pl.ds(t, rows, stride=chunk), :] = x_ref[:, (k * chunk + t) * LANES:(k * chunk + t + 1) * LANES]
    return jnp.concatenate([scr_ref[k] for k in range(n_tiles)], axis=1)


def _proj_kernel(h_ref, g_ref, w_ref, *refs, scales, widths, chunk_rows):
    out_refs = refs[:len(scales)]
    hn = _rms(h_ref[...], g_ref[...]).astype(BF16)
    off = 0
    for i, (o_ref, s, n) in enumerate(zip(out_refs, scales, widths)):
        acc = jnp.dot(hn, w_ref[:, off:off + n], preferred_element_type=F32)
        if s != 1.0:
            acc = acc * s
        if i == 0 and chunk_rows:
            _to_chunk_rows(acc, o_ref, refs[-1], chunk_rows)
        else:
            o_ref[...] = acc.astype(o_ref.dtype)
        off += n


def _norm_proj(h, g, w, outs, chunk_rows=0):
    m, d = h.shape
    tm = PROJ_BLOCK
    widths = tuple(n for n, _, _ in outs)
    kern = functools.partial(_proj_kernel, scales=tuple(s for _, _, s in outs), widths=widths,
                             chunk_rows=chunk_rows)
    out_specs = [pl.BlockSpec((tm, n), lambda i: (i, 0)) for n in widths]
    out_shape = [jax.ShapeDtypeStruct((m, n), dt) for n, dt, _ in outs]
    scratch = []
    if chunk_rows:
        n0 = widths[0]
        out_specs[0] = pl.BlockSpec((tm // chunk_rows, n0 * chunk_rows), lambda i: (i, 0))
        out_shape[0] = jax.ShapeDtypeStruct((m // chunk_rows, n0 * chunk_rows), outs[0][1])
        scratch = [pltpu.VMEM((n0 // LANES, tm, LANES), F32)]
    return pl.pallas_call(
        kern,
        grid=(m // tm,),
        in_specs=[pl.BlockSpec((tm, d), lambda i: (i, 0)),
                  _const_spec((1, d)),
                  _const_spec(w.shape)],
        out_specs=out_specs,
        out_shape=out_shape,
        scratch_shapes=scratch,
        compiler_params=_cparams("parallel"),
        name="norm_proj",
    )(h, g.reshape(1, d), w)


def _cmul(ar, ai, br, bi):
    return ar * br - ai * bi, ar * bi + ai * br


def _s5_discretize(lam_re, lam_im, dt):
    lr = jnp.minimum(lam_re, LAM_RE_MAX)
    li = lam_im
    mag = jnp.exp(lr * dt)
    ar = mag * jnp.cos(li * dt)
    ai = mag * jnp.sin(li * dt)
    den = lr * lr + li * li
    xr = ar - 1.0
    zr = (xr * lr + ai * li) / den
    zi = (ai * lr - xr * li) / den
    return ar, ai, zr, zi


def _s5_kernel(u_ref, lam_re_ref, lam_im_ref, logdt_ref, ct_re_ref, ct_im_ref, b_re_ref, b_im_ref,
               lamr_re_ref, lamr_im_ref, logdtr_ref, bt_re_ref, bt_im_ref, d_ref,
               lams_re_ref, lams_im_ref, logdts_ref,
               y_ref, m_ref, qt_ref, pm_ref, *, chunk, n_chan):
    t, hh = chunk, n_chan
    tw = t * LANES
    p = lam_re_ref.shape[1]
    gt = LANES // hh
    sw = gt * p
    nc = u_ref.shape[0]
    n_cbits = max(1, (nc - 1).bit_length())

    @pl.when(pl.program_id(1) == 0)
    def _():
        dt = jnp.exp(logdt_ref[0])
        ar, ai, zr, zi = _s5_discretize(lam_re_ref[0], lam_im_ref[0], dt)
        pw = [(jnp.ones((p, LANES), F32), jnp.zeros((p, LANES), F32))]
        for _ in range(t):
            pw.append(_cmul(pw[-1][0], pw[-1][1], ar, ai))
        bbr, bbi = _cmul(zr, zi, b_re_ref[0], b_im_ref[0])
        ctr, cti = ct_re_ref[0], ct_im_ref[0]
        lane_g = lax.broadcasted_iota(jnp.int32, (p, LANES), 1) // hh
        zero = jnp.zeros((p, LANES), F32)
        q_slots = [_cmul(pw[t - 1 - s][0], pw[t - 1 - s][1], bbr, bbi) for s in range(t)]
        c_slots = [_cmul(ctr, cti, pw[s + 1][0], pw[s + 1][1]) for s in range(t)]
        for g2 in range(gt):
            own = lane_g == g2
            rows_re = slice(g2 * p, (g2 + 1) * p)
            rows_im = slice(sw + g2 * p, sw + (g2 + 1) * p)
            qt_ref[rows_re, :] = jnp.concatenate([jnp.where(own, r, zero) for r, _ in q_slots], axis=1).astype(BF16)
            qt_ref[rows_im, :] = jnp.concatenate([jnp.where(own, i, zero) for _, i in q_slots], axis=1).astype(BF16)
            pm_ref[rows_re, :] = jnp.concatenate([jnp.where(own, r, zero) for r, _ in c_slots], axis=1).astype(BF16)
            pm_ref[rows_im, :] = jnp.concatenate([jnp.where(own, -i, zero) for _, i in c_slots], axis=1).astype(BF16)

        _, _, zr2, zi2 = _s5_discretize(lamr_re_ref[0], lamr_im_ref[0], jnp.exp(logdtr_ref[0]))
        first = lax.broadcasted_iota(jnp.int32, (LANES, 2 * p), 1) < p
        btr, bti = bt_re_ref[0], bt_im_ref[0]
        lhs = jnp.where(first, zr2 * btr - zi2 * bti, -(zr2 * bti + zi2 * btr))
        rhs = jnp.concatenate(
            [jnp.concatenate([_cmul(ctr, cti, pw[tau][0], pw[tau][1])[0] for tau in range(t)], axis=1),
             jnp.concatenate([_cmul(ctr, cti, pw[tau][0], pw[tau][1])[1] for tau in range(t)], axis=1)],
            axis=0)
        lh, ll = lhs.astype(BF16), (lhs - lhs.astype(BF16).astype(F32)).astype(BF16)
        rh, rl = rhs.astype(BF16), (rhs - rhs.astype(BF16).astype(F32)).astype(BF16)
        z = (jnp.dot(lh, rh, preferred_element_type=F32)
             + jnp.dot(lh, rl, preferred_element_type=F32)
             + jnp.dot(ll, rh, preferred_element_type=F32))
        row_z = lax.broadcasted_iota(jnp.int32, (LANES, tw), 0)
        lane_z = lax.broadcasted_iota(jnp.int32, (LANES, tw), 1)
        z = jnp.where(row_z // hh == (lane_z % LANES) // hh, z, 0.0)
        d_diag = jnp.where(lax.broadcasted_iota(jnp.int32, (LANES, LANES), 0)
                           == lax.broadcasted_iota(jnp.int32, (LANES, LANES), 1), d_ref[0], 0.0)
        z = jnp.concatenate([z[:, 0:LANES] + d_diag, z[:, LANES:]], axis=1)
        zb = z.astype(BF16)
        for s_in in range(t):
            if s_in:
                m_ref[s_in * LANES:(s_in + 1) * LANES, 0:s_in * LANES] = jnp.zeros((LANES, s_in * LANES), BF16)
            m_ref[s_in * LANES:(s_in + 1) * LANES, s_in * LANES:tw] = zb[:, 0:tw - s_in * LANES]

    u = u_ref[...].astype(BF16)
    y = jnp.concatenate(
        [jnp.dot(u[:, 0:(i + 2) * LANES], m_ref[0:(i + 2) * LANES, i * LANES:(i + 2) * LANES],
                 preferred_element_type=F32) for i in range(0, t, 2)], axis=1)
    x = lax.dot_general(u, qt_ref[...], (((1,), (1,)), ((), ())), preferred_element_type=F32)

    mr, mi, _, _ = _s5_discretize(lams_re_ref[0], lams_im_ref[0], jnp.exp(logdts_ref[0]))
    for _ in range(t.bit_length() - 1):
        mr, mi = _cmul(mr, mi, mr, mi)
    ridx = lax.broadcasted_iota(jnp.int32, (nc, sw), 0)

    def shift_down(v, k):
        if k % 8 == 0:
            return jnp.concatenate([jnp.zeros((k, sw), F32), v[:nc - k]], axis=0)
        return jnp.where(ridx >= k, pltpu.roll(v, k, axis=0), 0.0)

    s_re, s_im = x[:, 0:sw], x[:, sw:2 * sw]
    for k in range(n_cbits):
        h_re, h_im = shift_down(s_re, 1 << k), shift_down(s_im, 1 << k)
        s_re, s_im = s_re + (mr * h_re - mi * h_im), s_im + (mr * h_im + mi * h_re)
        mr, mi = _cmul(mr, mi, mr, mi)
    s_prev = jnp.concatenate([shift_down(s_re, 1), shift_down(s_im, 1)], axis=1).astype(BF16)
    y_ref[...] = y + jnp.dot(s_prev, pm_ref[...], preferred_element_type=F32)


def _s5_mix(u, lam_re, lam_im, log_dt, b_re, b_im, c_re, c_im, d_skip, n_batch):
    rows, _ = u.shape
    g, p, hh = b_re.shape
    t = S5_CHUNK
    gt = LANES // hh
    nt = g // gt
    tw = t * LANES
    nc = rows // n_batch

    def lanes_gh(a):
        return a.reshape(nt, gt, p, hh).transpose(0, 2, 1, 3).reshape(nt, p, LANES)

    def rows_gh(a):
        r = a.reshape(nt, gt, p, hh).transpose(0, 1, 3, 2).reshape(nt, LANES, p)
        return jnp.concatenate([r, r], axis=-1)

    per_gh = lambda a: jnp.broadcast_to(a.reshape(g, p, 1), (g, p, hh))
    lam_re_c, lam_im_c = lanes_gh(per_gh(lam_re)), lanes_gh(per_gh(lam_im))
    logdt_c = jnp.broadcast_to(log_dt.reshape(nt, gt, 1), (nt, gt, hh)).reshape(nt, 1, LANES)
    ct_re, ct_im = lanes_gh(c_re.transpose(0, 2, 1)), lanes_gh(c_im.transpose(0, 2, 1))
    b_re_c, b_im_c = lanes_gh(b_re), lanes_gh(b_im)
    lamr = lambda a: rows_gh(jnp.broadcast_to(a.reshape(g, p, 1), (g, p, hh)))
    logdt_r = jnp.broadcast_to(log_dt.reshape(nt, gt, 1), (nt, gt, hh)).reshape(nt, LANES, 1)
    d_c = d_skip.reshape(nt, 1, LANES)
    lams = lambda a: a.reshape(nt, 1, gt * p)
    logdt_s = jnp.broadcast_to(log_dt.reshape(nt, gt, 1), (nt, gt, p)).reshape(nt, 1, gt * p)

    params = [lam_re_c, lam_im_c, logdt_c, ct_re, ct_im, b_re_c, b_im_c,
              lamr(lam_re), lamr(lam_im), logdt_r, rows_gh(b_re), rows_gh(b_im), d_c,
              lams(lam_re), lams(lam_im), logdt_s]
    pspec = lambda a: pl.BlockSpec((1,) + a.shape[1:], lambda k, b: (k,) + (0,) * (a.ndim - 1))
    kern = functools.partial(_s5_kernel, chunk=t, n_chan=hh)
    return pl.pallas_call(
        kern,
        grid=(nt, n_batch),
        in_specs=[pl.BlockSpec((nc, tw), lambda k, b: (b, k))] + [pspec(a) for a in params],
        out_specs=pl.BlockSpec((nc, tw), lambda k, b: (b, k)),
        out_shape=jax.ShapeDtypeStruct(u.shape, F32),
        scratch_shapes=[pltpu.VMEM((tw, tw), BF16),
                        pltpu.VMEM((2 * gt * p, tw), BF16),
                        pltpu.VMEM((2 * gt * p, tw), BF16)],
        compiler_params=_cparams("arbitrary", "arbitrary"),
        name="s5_mix",
    )(u, *params)


HG_LOW_LEVELS = 3


def _hgrn_kernel(x_ref, lbl_ref, hn_ref, o_ref, st_ref, *, layer, n_heads, chunk):
    c = chunk
    n_sub = x_ref.shape[0] // c
    width = o_ref.shape[-1]
    dk = width // n_heads
    n_lev = c.bit_length() - 1

    @pl.when(pl.program_id(1) == 0)
    def _():
        st_ref[...] = jnp.zeros_like(st_ref)

    lg = lbl_ref[...]
    e = jnp.exp(lg - jnp.max(lg, axis=0, keepdims=True))
    sm = e / jnp.sum(e, axis=0, keepdims=True)
    lb = jnp.zeros((1, width), F32)
    for r in range(1, layer + 1):
        lb = lb + sm[r:r + 1]
    lb_floor = jnp.maximum(lb, 1e-30)
    one_m_lb = 1.0 - lb
    hnw = hn_ref[...]

    ri = lax.broadcasted_iota(jnp.int32, (c, c), 0)
    ci = lax.broadcasted_iota(jnp.int32, (c, c), 1)
    tri = jnp.where(ci <= ri, 1.0, 0.0).astype(BF16)
    keeps, lows = [], []
    for l in range(1, n_lev + 1):
        blk, half = 1 << l, 1 << (l - 1)
        keeps.append((ri // blk == ci // blk) & (ri % blk >= half) & (ci % blk < half))
        if l <= HG_LOW_LEVELS:
            bd = (ri // blk) * blk + half - 1
            lows.append(jnp.where(((ci > bd) & (ci <= ri)) | ((ci > ri) & (ci <= bd)), 1.0, 0.0))
    w_low = jnp.concatenate(lows, axis=0).astype(BF16)
    diag = ri == ci

    nt = (((1,), (1,)), ((), ()))
    tn = (((0,), (0,)), ((), ()))
    def decays(sub):
        rows = slice(sub * c, (sub + 1) * c)
        q = x_ref[rows, 0:width]
        zf = x_ref[rows, width:2 * width]
        sig = 1.0 / (1.0 + jnp.exp(-zf))
        logf = jnp.log(lb_floor + one_m_lb * sig)
        k = one_m_lb * (1.0 - sig)
        hi = logf.astype(BF16)
        mid = (logf - hi.astype(F32)).astype(BF16)
        b = jnp.dot(tri, hi, preferred_element_type=F32) + jnp.dot(tri, mid, preferred_element_type=F32)
        ex_low = (jnp.dot(w_low, hi, preferred_element_type=F32)
                  + jnp.dot(w_low, mid, preferred_element_type=F32))
        return q, k, b, ex_low

    def mix(sub, q, k, b, ex_low):
        rows = slice(sub * c, (sub + 1) * c)
        v = x_ref[rows, 2 * width:3 * width]
        gate = x_ref[rows, 3 * width:4 * width]
        qb, kb = q.astype(BF16), k.astype(BF16)
        att = [jnp.where(diag, jnp.sum((q * k)[:, h * dk:(h + 1) * dk], axis=-1, keepdims=True), 0.0)
               for h in range(n_heads)]
        for l in range(1, n_lev + 1):
            blk, half = 1 << l, 1 << (l - 1)
            if l <= HG_LOW_LEVELS:
                ex = ex_low[(l - 1) * c:l * c]
            else:
                pieces = []
                for s0 in range(0, c, blk):
                    bd = b[s0 + half - 1:s0 + half]
                    pieces.append(bd - b[s0:s0 + half])
                    pieces.append(b[s0 + half:s0 + blk] - bd)
                ex = jnp.concatenate(pieces, axis=0)
            wgt = jnp.exp(ex).astype(BF16)
            ql, kl = qb * wgt, kb * wgt
            for h in range(n_heads):
                sl = slice(h * dk, (h + 1) * dk)
                pm = lax.dot_general(ql[:, sl], kl[:, sl], nt, preferred_element_type=F32)
                att[h] = att[h] + jnp.where(keeps[l - 1], pm, 0.0)

        b_last = b[c - 1:c]
        q_in = (q * jnp.exp(b)).astype(BF16)
        k_out = (k * jnp.exp(b_last - b)).astype(BF16)
        e_last = jnp.exp(b_last)
        vb = v.astype(BF16)
        for h in range(n_heads):
            sl = slice(h * dk, (h + 1) * dk)
            st = st_ref[h]
            o = lax.dot_general(q_in[:, sl], st.astype(BF16), nt, preferred_element_type=F32)
            o = o + jnp.dot(att[h].astype(BF16), vb[:, sl], preferred_element_type=F32)
            st_ref[h] = st * e_last[:, sl] + lax.dot_general(vb[:, sl], k_out[:, sl], tn,
                                                              preferred_element_type=F32)
            o = o * lax.rsqrt(jnp.mean(o * o, axis=-1, keepdims=True) + EPS) * hnw[:, sl]
            gh = gate[:, sl]
            o_ref[rows, sl] = o * (gh * _sigmoid(gh))

    nxt = decays(0)
    for sub in range(n_sub):
        cur = nxt
        if sub + 1 < n_sub:
            nxt = decays(sub + 1)
        mix(sub, *cur)


def _hgrn_mix(x, lb_logits, head_norm, layer, n_batch):
    m = x.shape[0]
    n_layers, width = lb_logits.shape
    n_heads = head_norm.shape[0]
    dk = width // n_heads
    rows = HG_BLOCK
    per = m // n_batch // rows
    kern = functools.partial(_hgrn_kernel, layer=layer, n_heads=n_heads, chunk=HG_CHUNK)
    return pl.pallas_call(
        kern,
        grid=(n_batch, per),
        in_specs=[pl.BlockSpec((rows, 4 * width), lambda b, i: (b * per + i, 0)),
                  _const_spec((n_layers, width)),
                  _const_spec((1, width))],
        out_specs=pl.BlockSpec((rows, width), lambda b, i: (b * per + i, 0)),
        out_shape=jax.ShapeDtypeStruct((m, width), F32),
        scratch_shapes=[pltpu.VMEM((n_heads, dk, dk), F32)],
        compiler_params=_cparams("parallel", "arbitrary"),
        name="hgrn_mix",
    )(x, lb_logits, head_norm.reshape(1, width))


_STACK = (0, 2, 1, 3)


def _attn_kernel(sink_ref, q_ref, kv_ref, kvp_ref, o_ref, bias_ref, sinkc_ref, *, n_q, n_kv):
    tq = q_ref.shape[0]
    hd = q_ref.shape[1] // n_q
    grp = n_q // n_kv
    w = WINDOW
    lanes = 2 * hd
    nkw = n_kv * lanes
    first_block = pl.program_id(1) == 0

    @pl.when((pl.program_id(0) == 0) & first_block)
    def _():
        rr = lax.broadcasted_iota(jnp.int32, (2 * w, grp * w), 0)
        cc = lax.broadcasted_iota(jnp.int32, (2 * w, grp * w), 1)
        dist = cc % w + w - rr
        valid = (dist >= 0) & (dist < w)
        distf = dist.astype(F32)
        pos = cc // w
        posr = lax.broadcasted_iota(jnp.int32, (8, grp * w), 1) // w
        for kh in range(n_kv):
            slope = jnp.zeros((2 * w, grp * w), F32)
            sink = jnp.zeros((8, grp * w), F32)
            for i, g in enumerate(_STACK):
                head = kh * grp + g
                slope = jnp.where(pos == i, _LOG2E * 2.0 ** (-8.0 * (head + 1) / n_q), slope)
                sink = jnp.where(posr == i, _LOG2E * sink_ref[head], sink)
            b = jnp.where(valid, -slope * distf, NEG_BIG)
            bias_ref[0, kh] = b
            bias_ref[1, kh] = jnp.where(rr >= w, b, NEG_BIG)
            sinkc_ref[kh] = sink

    lane_q = lax.broadcasted_iota(jnp.int32, (w, lanes), 1)
    lo_q = lane_q < hd
    lane_v = lax.broadcasted_iota(jnp.int32, (2 * w, lanes), 1)
    lo_v = lane_v < hd
    zq = jnp.zeros((w, lanes), BF16)
    zv = jnp.zeros((2 * w, lanes), BF16)
    first_idx = jnp.where(first_block, 1, 0)

    nt = (((1,), (1,)), ((), ()))
    tn = (((0,), (0,)), ((), ()))

    def scores(jb, kh):
        rs = slice(jb * w, (jb + 1) * w)
        ks = slice(kh * lanes, (kh + 1) * lanes)
        if jb == 0:
            kprev = kvp_ref[:, ks]
            bias = bias_ref[first_idx, kh]
        else:
            kprev = kv_ref[(jb - 1) * w:jb * w, ks]
            bias = bias_ref[0, kh]
        kw = jnp.concatenate([kprev, kv_ref[rs, ks]], axis=0)
        qa = q_ref[rs, (2 * kh) * lanes:(2 * kh + 1) * lanes]
        qb = q_ref[rs, (2 * kh + 1) * lanes:(2 * kh + 2) * lanes]
        qs = jnp.concatenate([jnp.where(lo_q, qa, zq), jnp.where(lo_q, qb, zq),
                              jnp.where(lo_q, zq, qa), jnp.where(lo_q, zq, qb)], axis=0)
        return lax.dot_general(kw, qs, nt, preferred_element_type=F32) + bias

    def finish(jb, kh, st):
        rs = slice(jb * w, (jb + 1) * w)
        vs = slice(nkw + kh * lanes, nkw + (kh + 1) * lanes)
        vprev = kvp_ref[:, vs] if jb == 0 else kv_ref[(jb - 1) * w:jb * w, vs]
        vw = jnp.concatenate([vprev, kv_ref[rs, vs]], axis=0)
        sink = sinkc_ref[kh][0:1]
        mx = jnp.maximum(jnp.max(st, axis=0, keepdims=True), sink)
        pe = jnp.exp2(st - mx)
        den = jnp.sum(pe, axis=0, keepdims=True) + jnp.exp2(sink - mx)
        pn = (pe * (1.0 / den)).astype(BF16)
        o = (lax.dot_general(pn[:, 0:2 * w], jnp.where(lo_v, vw, zv), tn, preferred_element_type=F32)
             + lax.dot_general(pn[:, 2 * w:4 * w], jnp.where(lo_v, zv, vw), tn, preferred_element_type=F32))
        o_ref[rs, (2 * kh) * lanes:(2 * kh + 1) * lanes] = o[0:w].astype(o_ref.dtype)
        o_ref[rs, (2 * kh + 1) * lanes:(2 * kh + 2) * lanes] = o[w:2 * w].astype(o_ref.dtype)

    items = [(jb, kh) for jb in range(tq // w) for kh in range(n_kv)]
    pending = [scores(*items[i]) for i in range(ATTN_AHEAD)]
    for idx, item in enumerate(items):
        st = pending.pop(0)
        if idx + ATTN_AHEAD < len(items):
            pending.append(scores(*items[idx + ATTN_AHEAD]))
        finish(*item, st)


def _attention(q, kv, sinks, n_batch):
    m, qw = q.shape
    n_q = sinks.shape[0]
    hd = qw // n_q
    n_kv = kv.shape[1] // (4 * hd)
    grp = n_q // n_kv
    assert grp == 4 and 2 * hd == 128
    tq = ATTN_BLOCK_Q
    per = m // n_batch // tq
    ratio = tq // WINDOW
    kern = functools.partial(_attn_kernel, n_q=n_q, n_kv=n_kv)
    return pl.pallas_call(
        kern,
        grid=(n_batch, per),
        in_specs=[pl.BlockSpec(memory_space=pltpu.SMEM),
                  pl.BlockSpec((tq, qw), lambda b, i: (b * per + i, 0)),
                  pl.BlockSpec((tq, kv.shape[1]), lambda b, i: (b * per + i, 0)),
                  pl.BlockSpec((WINDOW, kv.shape[1]),
                               lambda b, i: (jnp.maximum((b * per + i) * ratio - 1, 0), 0))],
        out_specs=pl.BlockSpec((tq, qw), lambda b, i: (b * per + i, 0)),
        out_shape=jax.ShapeDtypeStruct((m, qw), BF16),
        scratch_shapes=[pltpu.VMEM((2, n_kv, 2 * WINDOW, grp * WINDOW), F32),
                        pltpu.VMEM((n_kv, 8, grp * WINDOW), F32)],
        compiler_params=_cparams("arbitrary", "arbitrary"),
        name="swa_attention",
    )(sinks, q, kv, kv)


def _dup_heads(wcols, n_heads):
    d = wcols.shape[0]
    wh = wcols.reshape(d, n_heads, -1)
    return jnp.concatenate([wh, wh], axis=-1).reshape(d, -1)


def _tail_kernel(*refs, even, final, ff_block):
    refs = list(refs)
    n_out = len(refs) - 1 if even else len(refs)
    o_ref = refs.pop(n_out - 1)
    h_ref = refs.pop(0)
    if even:
        ya_ref, yb_ref, wglu_ref, bglu_ref = refs[:4]
        refs = refs[4:]
    else:
        a_ref = refs.pop(0)
    p_ref, wout_ref, g2_ref, w1_ref, w2_ref, g3_ref, wup_ref, wgate_ref = refs[:8]
    gf_ref = refs[8] if final else None

    tm = h_ref.shape[0]
    halves = [slice(i * tm // TAIL_STREAMS, (i + 1) * tm // TAIL_STREAMS) for i in range(TAIL_STREAMS)]
    hs = [h_ref[r, :] for r in halves]
    if even:
        ya = _from_chunk_rows(ya_ref, refs[-1], S5_CHUNK)
        zs = [jax.nn.gelu(ya[r]) for r in halves]
        oas = [z * _sigmoid(_bdot(z, wglu_ref[...]) + bglu_ref[...]) for z in zs]
        wa = oas[0].shape[-1]
        mixes = [_bdot(oa, wout_ref[0:wa, :]) + _bdot(yb_ref[r, :], wout_ref[wa:, :]) for oa, r in zip(oas, halves)]
    else:
        mixes = [_bdot(a_ref[r, :], wout_ref[...]) for r in halves]
    hs = [h + mix for h, mix in zip(hs, mixes)]

    hns = [_rms(h, g2_ref[...]).astype(BF16) for h in hs]
    d_ff = w1_ref.shape[1]
    accs = [jnp.zeros_like(h) for h in hs]
    for c0 in range(0, d_ff, ff_block):
        for i, hn in enumerate(hns):
            a = jnp.dot(hn, w1_ref[:, c0:c0 + ff_block], preferred_element_type=F32)
            a = jnp.square(jnp.maximum(a, 0.0)).astype(BF16)
            accs[i] = accs[i] + jnp.dot(a, w2_ref[c0:c0 + ff_block, :], preferred_element_type=F32)
    hs = [h + acc for h, acc in zip(hs, accs)]

    hns = [_rms(h, g3_ref[...]) for h in hs]
    gates = [_sigmoid(_bdot(hn, wgate_ref[...])) for hn in hns]
    hs = [h + _bdot(p_ref[r, :], wup_ref[...]) * gate for h, gate, r in zip(hs, gates, halves)]
    for h, r in zip(hs, halves):
        o_ref[r, :] = _rms(h, gf_ref[...]) if final else h


def _tail(h, mixer_ins, mixer_weights, p, layer, w_out, g2, w1, w2, g3, w_up, w_gate, g_final, even):
    m, d = h.shape
    tm = TOKEN_BLOCK
    tok = lambda a: pl.BlockSpec((tm, a.shape[1]), lambda i: (i, 0))
    rowv = lambda a: a.reshape(1, -1)
    weights = [w_out, rowv(g2), w1, w2, rowv(g3), w_up, w_gate]
    final = g_final is not None
    if final:
        weights.append(rowv(g_final))
    args = [h, *mixer_ins, *mixer_weights, p, *weights]
    specs = ([tok(h)] + [tok(a) for a in mixer_ins] + [_const_spec(a.shape) for a in mixer_weights]
             + [pl.BlockSpec((None, tm, p.shape[-1]), lambda i: (layer, i, 0))] + [_const_spec(a.shape) for a in weights])
    scratch = []
    if even:
        ya = mixer_ins[0]
        specs[1] = pl.BlockSpec((tm // S5_CHUNK, ya.shape[1]), lambda i: (i, 0))
        scratch = [pltpu.VMEM((ya.shape[1] // S5_CHUNK // LANES, tm, LANES), F32)]
    kern = functools.partial(_tail_kernel, even=even, final=final, ff_block=1024)
    return pl.pallas_call(
        kern,
        grid=(m // tm,),
        in_specs=specs,
        out_specs=pl.BlockSpec((tm, d), lambda i: (i, 0)),
        out_shape=jax.ShapeDtypeStruct((m, d), F32),
        scratch_shapes=scratch,
        compiler_params=_cparams("parallel"),
        name="layer_tail",
    )(*args)


def kernel(x, p, mix_norm, mlp_norm, ple_norm, final_norm, w_in_even, w_out_even, s5_lam_re, s5_lam_im, s5_log_dt, s5_b_re, s5_b_im, s5_c_re, s5_c_im, s5_d, s5_w_glu, s5_b_glu, hgrn_lb_logits, hgrn_norm, w_qkv_odd, w_o_odd, attn_sinks, w_mlp_in, w_mlp_out, w_ple_up, w_ple_gate):
    bsz, seq, d = x.shape
    depth = p.shape[0]
    m = bsz * seq
    s5_w = s5_w_glu.shape[-1]
    hg_w = hgrn_lb_logits.shape[-1]
    n_q = attn_sinks.shape[-1]
    hd = d // n_q
    kv_w = w_qkv_odd.shape[-1] - n_q * hd
    assert seq % max(TOKEN_BLOCK, ATTN_BLOCK_Q, HG_BLOCK, S5_CHUNK) == 0 and m % PROJ_BLOCK == 0

    h = x.reshape(m, d)
    for i in range(depth):
        j = i // 2
        if i % 2 == 0:
            u, hg = _norm_proj(h, mix_norm[i], w_in_even[j].astype(BF16),
                               [(s5_w, F32, 1.0), (4 * hg_w, F32, 1.0)], chunk_rows=S5_CHUNK)
            ya = _s5_mix(u, s5_lam_re[j], s5_lam_im[j], s5_log_dt[j], s5_b_re[j], s5_b_im[j],
                         s5_c_re[j], s5_c_im[j], s5_d[j], bsz)
            yb = _hgrn_mix(hg, hgrn_lb_logits, hgrn_norm[j], j, bsz)
            mixer_ins = [ya, yb]
            mixer_weights = [s5_w_glu[j].astype(BF16), s5_b_glu[j].reshape(1, -1)]
            w_out = w_out_even[j]
        else:
            wq, wk, wv = jnp.split(w_qkv_odd[j], [n_q * hd, n_q * hd + kv_w // 2], axis=-1)
            n_kv = kv_w // (2 * hd)
            w_qkv = jnp.concatenate([wq, _dup_heads(wk, n_kv), _dup_heads(wv, n_kv)], axis=-1)
            q, kv = _norm_proj(h, mix_norm[i], w_qkv.astype(BF16),
                               [(n_q * hd, BF16, _LOG2E / math.sqrt(hd)), (2 * kv_w, BF16, 1.0)])
            mixer_ins = [_attention(q, kv, attn_sinks[j], bsz)]
            mixer_weights = []
            w_out = w_o_odd[j]
        h = _tail(h, mixer_ins, mixer_weights, p.reshape(depth, m, -1), i, w_out.astype(BF16),
                  mlp_norm[i], w_mlp_in[i].astype(BF16), w_mlp_out[i].astype(BF16),
                  ple_norm[i], w_ple_up[i].astype(BF16), w_ple_gate[i].astype(BF16),
                  final_norm if i == depth - 1 else None, even=(i % 2 == 0))
    return h.reshape(bsz, seq, d)
```

```python
import functools
import math

import jax
import jax.numpy as jnp
from jax import lax
from jax.experimental import pallas as pl
from jax.experimental.pallas import tpu as pltpu

F32 = jnp.float32
BF16 = jnp.bfloat16

EPS = 1e-6
WINDOW = 128
S5_CHUNK = 16
LANES = 128
HG_CHUNK = 128
HG_BLOCK = 1024
LAM_RE_MAX = -1e-4
NEG_BIG = -1e30
_LOG2E = math.log2(math.e)

_V7X_VMEM_BYTES = 64 * 1024 * 1024
_VMEM_LIMIT = _V7X_VMEM_BYTES - 12 * 1024 * 1024

TOKEN_BLOCK = 512
TAIL_STREAMS = 2
PROJ_BLOCK = 1024
ATTN_BLOCK_Q = 1024
ATTN_AHEAD = 2


def _cparams(*sem):
    return pltpu.CompilerParams(dimension_semantics=sem, vmem_limit_bytes=_VMEM_LIMIT)


def _const_spec(shape):
    nd = len(shape)
    return pl.BlockSpec(shape, lambda *_: (0,) * nd, pipeline_mode=pl.Buffered(1))


def _rms(x, g):
    ms = jnp.mean(x * x, axis=-1, keepdims=True)
    return x * lax.rsqrt(ms + EPS) * g


def _sigmoid(x):
    return 1.0 / (1.0 + jnp.exp(-x))


def _bdot(a, b):
    return jnp.dot(a.astype(BF16), b.astype(BF16), preferred_element_type=F32)


def _to_chunk_rows(val, o_ref, scr_ref, chunk):
    n_tiles = scr_ref.shape[0]
    rows = o_ref.shape[0]
    for k in range(n_tiles):
        scr_ref[k] = val[:, k * LANES:(k + 1) * LANES]
    for k in range(n_tiles):
        for t in range(chunk):
            o_ref[:, (k * chunk + t) * LANES:(k * chunk + t + 1) * LANES] = scr_ref[k, pl.ds(t, rows, stride=chunk), :]


def _from_chunk_rows(x_ref, scr_ref, chunk):
    n_tiles = scr_ref.shape[0]
    rows = x_ref.shape[0]
    for k in range(n_tiles):
        for t in range(chunk):
            scr_ref[k, pl.ds(t, rows, stride=chunk), :] = x_ref[:, (k * chunk + t) * LANES:(k * chunk + t + 1) * LANES]
    return jnp.concatenate([scr_ref[k] for k in range(n_tiles)], axis=1)


def _proj_kernel(h_ref, g_ref, w_ref, *refs, scales, widths, chunk_rows):
    out_refs = refs[:len(scales)]
    hn = _rms(h_ref[...], g_ref[...]).astype(BF16)
    off = 0
    for i, (o_ref, s, n) in enumerate(zip(out_refs, scales, widths)):
        acc = jnp.dot(hn, w_ref[:, off:off + n], preferred_element_type=F32)
        if s != 1.0:
            acc = acc * s
        if i == 0 and chunk_rows:
            _to_chunk_rows(acc, o_ref, refs[-1], chunk_rows)
        else:
            o_ref[...] = acc.astype(o_ref.dtype)
        off += n


def _norm_proj(h, g, w, outs, chunk_rows=0):
    m, d = h.shape
    tm = PROJ_BLOCK
    widths = tuple(n for n, _, _ in outs)
    kern = functools.partial(_proj_kernel, scales=tuple(s for _, _, s in outs), widths=widths,
                             chunk_rows=chunk_rows)
    out_specs = [pl.BlockSpec((tm, n), lambda i: (i, 0)) for n in widths]
    out_shape = [jax.ShapeDtypeStruct((m, n), dt) for n, dt, _ in outs]
    scratch = []
    if chunk_rows:
        n0 = widths[0]
        out_specs[0] = pl.BlockSpec((tm // chunk_rows, n0 * chunk_rows), lambda i: (i, 0))
        out_shape[0] = jax.ShapeDtypeStruct((m // chunk_rows, n0 * chunk_rows), outs[0][1])
        scratch = [pltpu.VMEM((n0 // LANES, tm, LANES), F32)]
    return pl.pallas_call(
        kern,
        grid=(m // tm,),
        in_specs=[pl.BlockSpec((tm, d), lambda i: (i, 0)),
                  _const_spec((1, d)),
                  _const_spec(w.shape)],
        out_specs=out_specs,
        out_shape=out_shape,
        scratch_shapes=scratch,
        compiler_params=_cparams("parallel"),
        name="norm_proj",
    )(h, g.reshape(1, d), w)


def _cmul(ar, ai, br, bi):
    return ar * br - ai * bi, ar * bi + ai * br


def _s5_discretize(lam_re, lam_im, dt):
    lr = jnp.minimum(lam_re, LAM_RE_MAX)
    li = lam_im
    mag = jnp.exp(lr * dt)
    ar = mag * jnp.cos(li * dt)
    ai = mag * jnp.sin(li * dt)
    den = lr * lr + li * li
    xr = ar - 1.0
    zr = (xr * lr + ai * li) / den
    zi = (ai * lr - xr * li) / den
    return ar, ai, zr, zi


def _s5_kernel(u_ref, lam_re_ref, lam_im_ref, logdt_ref, ct_re_ref, ct_im_ref, b_re_ref, b_im_ref,
               lamr_re_ref, lamr_im_ref, logdtr_ref, bt_re_ref, bt_im_ref, d_ref,
               lams_re_ref, lams_im_ref, logdts_ref,
               y_ref, m_ref, qt_ref, pm_ref, *, chunk, n_chan):
    t, hh = chunk, n_chan
    tw = t * LANES
    p = lam_re_ref.shape[1]
    gt = LANES // hh
    sw = gt * p
    nc = u_ref.shape[0]
    n_cbits = max(1, (nc - 1).bit_length())

    @pl.when(pl.program_id(1) == 0)
    def _():
        dt = jnp.exp(logdt_ref[0])
        ar, ai, zr, zi = _s5_discretize(lam_re_ref[0], lam_im_ref[0], dt)
        pw = [(jnp.ones((p, LANES), F32), jnp.zeros((p, LANES), F32))]
        for _ in range(t):
            pw.append(_cmul(pw[-1][0], pw[-1][1], ar, ai))
        bbr, bbi = _cmul(zr, zi, b_re_ref[0], b_im_ref[0])
        ctr, cti = ct_re_ref[0], ct_im_ref[0]
        lane_g = lax.broadcasted_iota(jnp.int32, (p, LANES), 1) // hh
        zero = jnp.zeros((p, LANES), F32)
        q_slots = [_cmul(pw[t - 1 - s][0], pw[t - 1 - s][1], bbr, bbi) for s in range(t)]
        c_slots = [_cmul(ctr, cti, pw[s + 1][0], pw[s + 1][1]) for s in range(t)]
        for g2 in range(gt):
            own = lane_g == g2
            rows_re = slice(g2 * p, (g2 + 1) * p)
            rows_im = slice(sw + g2 * p, sw + (g2 + 1) * p)
            qt_ref[rows_re, :] = jnp.concatenate([jnp.where(own, r, zero) for r, _ in q_slots], axis=1).astype(BF16)
            qt_ref[rows_im, :] = jnp.concatenate([jnp.where(own, i, zero) for _, i in q_slots], axis=1).astype(BF16)
            pm_ref[rows_re, :] = jnp.concatenate([jnp.where(own, r, zero) for r, _ in c_slots], axis=1).astype(BF16)
            pm_ref[rows_im, :] = jnp.concatenate([jnp.where(own, -i, zero) for _, i in c_slots], axis=1).astype(BF16)

        _, _, zr2, zi2 = _s5_discretize(lamr_re_ref[0], lamr_im_ref[0], jnp.exp(logdtr_ref[0]))
        first = lax.broadcasted_iota(jnp.int32, (LANES, 2 * p), 1) < p
        btr, bti = bt_re_ref[0], bt_im_ref[0]
        lhs = jnp.where(first, zr2 * btr - zi2 * bti, -(zr2 * bti + zi2 * btr))
        rhs = jnp.concatenate(
            [jnp.concatenate([_cmul(ctr, cti, pw[tau][0], pw[tau][1])[0] for tau in range(t)], axis=1),
             jnp.concatenate([_cmul(ctr, cti, pw[tau][0], pw[tau][1])[1] for tau in range(t)], axis=1)],
            axis=0)
        lh, ll = lhs.astype(BF16), (lhs - lhs.astype(BF16).astype(F32)).astype(BF16)
        rh, rl = rhs.astype(BF16), (rhs - rhs.astype(BF16).astype(F32)).astype(BF16)
        z = (jnp.dot(lh, rh, preferred_element_type=F32)
             + jnp.dot(lh, rl, preferred_element_type=F32)
             + jnp.dot(ll, rh, preferred_element_type=F32))
        row_z = lax.broadcasted_iota(jnp.int32, (LANES, tw), 0)
        lane_z = lax.broadcasted_iota(jnp.int32, (LANES, tw), 1)
        z = jnp.where(row_z // hh == (lane_z % LANES) // hh, z, 0.0)
        d_diag = jnp.where(lax.broadcasted_iota(jnp.int32, (LANES, LANES), 0)
                           == lax.broadcasted_iota(jnp.int32, (LANES, LANES), 1), d_ref[0], 0.0)
        z = jnp.concatenate([z[:, 0:LANES] + d_diag, z[:, LANES:]], axis=1)
        zb = z.astype(BF16)
        for s_in in range(t):
            if s_in:
                m_ref[s_in * LANES:(s_in + 1) * LANES, 0:s_in * LANES] = jnp.zeros((LANES, s_in * LANES), BF16)
            m_ref[s_in * LANES:(s_in + 1) * LANES, s_in * LANES:tw] = zb[:, 0:tw - s_in * LANES]

    u = u_ref[...].astype(BF16)
    y = jnp.concatenate(
        [jnp.dot(u[:, 0:(i + 2) * LANES], m_ref[0:(i + 2) * LANES, i * LANES:(i + 2) * LANES],
                 preferred_element_type=F32) for i in range(0, t, 2)], axis=1)
    x = lax.dot_general(u, qt_ref[...], (((1,), (1,)), ((), ())), preferred_element_type=F32)

    mr, mi, _, _ = _s5_discretize(lams_re_ref[0], lams_im_ref[0], jnp.exp(logdts_ref[0]))
    for _ in range(t.bit_length() - 1):
        mr, mi = _cmul(mr, mi, mr, mi)
    ridx = lax.broadcasted_iota(jnp.int32, (nc, sw), 0)

    def shift_down(v, k):
        if k % 8 == 0:
            return jnp.concatenate([jnp.zeros((k, sw), F32), v[:nc - k]], axis=0)
        return jnp.where(ridx >= k, pltpu.roll(v, k, axis=0), 0.0)

    s_re, s_im = x[:, 0:sw], x[:, sw:2 * sw]
    for k in range(n_cbits):
        h_re, h_im = shift_down(s_re, 1 << k), shift_down(s_im, 1 << k)
        s_re, s_im = s_re + (mr * h_re - mi * h_im), s_im + (mr * h_im + mi * h_re)
        mr, mi = _cmul(mr, mi, mr, mi)
    s_prev = jnp.concatenate([shift_down(s_re, 1), shift_down(s_im, 1)], axis=1).astype(BF16)
    y_ref[...] = y + jnp.dot(s_prev, pm_ref[...], preferred_element_type=F32)


def _s5_mix(u, lam_re, lam_im, log_dt, b_re, b_im, c_re, c_im, d_skip, n_batch):
    rows, _ = u.shape
    g, p, hh = b_re.shape
    t = S5_CHUNK
    gt = LANES // hh
    nt = g // gt
    tw = t * LANES
    nc = rows // n_batch

    def lanes_gh(a):
        return a.reshape(nt, gt, p, hh).transpose(0, 2, 1, 3).reshape(nt, p, LANES)

    def rows_gh(a):
        r = a.reshape(nt, gt, p, hh).transpose(0, 1, 3, 2).reshape(nt, LANES, p)
        return jnp.concatenate([r, r], axis=-1)

    per_gh = lambda a: jnp.broadcast_to(a.reshape(g, p, 1), (g, p, hh))
    lam_re_c, lam_im_c = lanes_gh(per_gh(lam_re)), lanes_gh(per_gh(lam_im))
    logdt_c = jnp.broadcast_to(log_dt.reshape(nt, gt, 1), (nt, gt, hh)).reshape(nt, 1, LANES)
    ct_re, ct_im = lanes_gh(c_re.transpose(0, 2, 1)), lanes_gh(c_im.transpose(0, 2, 1))
    b_re_c, b_im_c = lanes_gh(b_re), lanes_gh(b_im)
    lamr = lambda a: rows_gh(jnp.broadcast_to(a.reshape(g, p, 1), (g, p, hh)))
    logdt_r = jnp.broadcast_to(log_dt.reshape(nt, gt, 1), (nt, gt, hh)).reshape(nt, LANES, 1)
    d_c = d_skip.reshape(nt, 1, LANES)
    lams = lambda a: a.reshape(nt, 1, gt * p)
    logdt_s = jnp.broadcast_to(log_dt.reshape(nt, gt, 1), (nt, gt, p)).reshape(nt, 1, gt * p)

    params = [lam_re_c, lam_im_c, logdt_c, ct_re, ct_im, b_re_c, b_im_c,
              lamr(lam_re), lamr(lam_im), logdt_r, rows_gh(b_re), rows_gh(b_im), d_c,
              lams(lam_re), lams(lam_im), logdt_s]
    pspec = lambda a: pl.BlockSpec((1,) + a.shape[1:], lambda k, b: (k,) + (0,) * (a.ndim - 1))
    kern = functools.partial(_s5_kernel, chunk=t, n_chan=hh)
    return pl.pallas_call(
        kern,
        grid=(nt, n_batch),
        in_specs=[pl.BlockSpec((nc, tw), lambda k, b: (b, k))] + [pspec(a) for a in params],
        out_specs=pl.BlockSpec((nc, tw), lambda k, b: (b, k)),
        out_shape=jax.ShapeDtypeStruct(u.shape, F32),
        scratch_shapes=[pltpu.VMEM((tw, tw), BF16),
                        pltpu.VMEM((2 * gt * p, tw), BF16),
                        pltpu.VMEM((2 * gt * p, tw), BF16)],
        compiler_params=_cparams("arbitrary", "arbitrary"),
        name="s5_mix",
    )(u, *params)


HG_LOW_LEVELS = 3


def _hgrn_kernel(x_ref, lbl_ref, hn_ref, o_ref, st_ref, *, layer, n_heads, chunk):
    c = chunk
    n_sub = x_ref.shape[0] // c
    width = o_ref.shape[-1]
    dk = width // n_heads
    n_lev = c.bit_length() - 1

    @pl.when(pl.program_id(1) == 0)
    def _():
        st_ref[...] = jnp.zeros_like(st_ref)

    lg = lbl_ref[...]
    e = jnp.exp(lg - jnp.max(lg, axis=0, keepdims=True))
    sm = e / jnp.sum(e, axis=0, keepdims=True)
    lb = jnp.zeros((1, width), F32)
    for r in range(1, layer + 1):
        lb = lb + sm[r:r + 1]
    lb_floor = jnp.maximum(lb, 1e-30)
    one_m_lb = 1.0 - lb
    hnw = hn_ref[...]

    ri = lax.broadcasted_iota(jnp.int32, (c, c), 0)
    ci = lax.broadcasted_iota(jnp.int32, (c, c), 1)
    tri = jnp.where(ci <= ri, 1.0, 0.0).astype(BF16)
    keeps, lows = [], []
    for l in range(1, n_lev + 1):
        blk, half = 1 << l, 1 << (l - 1)
        keeps.append((ri // blk == ci // blk) & (ri % blk >= half) & (ci % blk < half))
        if l <= HG_LOW_LEVELS:
            bd = (ri // blk) * blk + half - 1
            lows.append(jnp.where(((ci > bd) & (ci <= ri)) | ((ci > ri) & (ci <= bd)), 1.0, 0.0))
    w_low = jnp.concatenate(lows, axis=0).astype(BF16)
    diag = ri == ci

    nt = (((1,), (1,)), ((), ()))
    tn = (((0,), (0,)), ((), ()))
    def decays(sub):
        rows = slice(sub * c, (sub + 1) * c)
        q = x_ref[rows, 0:width]
        zf = x_ref[rows, width:2 * width]
        sig = 1.0 / (1.0 + jnp.exp(-zf))
        logf = jnp.log(lb_floor + one_m_lb * sig)
        k = one_m_lb * (1.0 - sig)
        hi = logf.astype(BF16)
        mid = (logf - hi.astype(F32)).astype(BF16)
        b = jnp.dot(tri, hi, preferred_element_type=F32) + jnp.dot(tri, mid, preferred_element_type=F32)
        ex_low = (jnp.dot(w_low, hi, preferred_element_type=F32)
                  + jnp.dot(w_low, mid, preferred_element_type=F32))
        return q, k, b, ex_low

    def mix(sub, q, k, b, ex_low):
        rows = slice(sub * c, (sub + 1) * c)
        v = x_ref[rows, 2 * width:3 * width]
        gate = x_ref[rows, 3 * width:4 * width]
        qb, kb = q.astype(BF16), k.astype(BF16)
        att = [jnp.where(diag, jnp.sum((q * k)[:, h * dk:(h + 1) * dk], axis=-1, keepdims=True), 0.0)
               for h in range(n_heads)]
        for l in range(1, n_lev + 1):
            blk, half = 1 << l, 1 << (l - 1)
            if l <= HG_LOW_LEVELS:
                ex = ex_low[(l - 1) * c:l * c]
            else:
                pieces = []
                for s0 in range(0, c, blk):
                    bd = b[s0 + half - 1:s0 + half]
                    pieces.append(bd - b[s0:s0 + half])
                    pieces.append(b[s0 + half:s0 + blk] - bd)
                ex = jnp.concatenate(pieces, axis=0)
            wgt = jnp.exp(ex).astype(BF16)
            ql, kl = qb * wgt, kb * wgt
            for h in range(n_heads):
                sl = slice(h * dk, (h + 1) * dk)
                pm = lax.dot_general(ql[:, sl], kl[:, sl], nt, preferred_element_type=F32)
                att[h] = att[h] + jnp.where(keeps[l - 1], pm, 0.0)

        b_last = b[c - 1:c]
        q_in = (q * jnp.exp(b)).astype(BF16)
        k_out = (k * jnp.exp(b_last - b)).astype(BF16)
        e_last = jnp.exp(b_last)
        vb = v.astype(BF16)
        for h in range(n_heads):
            sl = slice(h * dk, (h + 1) * dk)
            st = st_ref[h]
            o = lax.dot_general(q_in[:, sl], st.astype(BF16), nt, preferred_element_type=F32)
            o = o + jnp.dot(att[h].astype(BF16), vb[:, sl], preferred_element_type=F32)
            st_ref[h] = st * e_last[:, sl] + lax.dot_general(vb[:, sl], k_out[:, sl], tn,
                                                              preferred_element_type=F32)
            o = o * lax.rsqrt(jnp.mean(o * o, axis=-1, keepdims=True) + EPS) * hnw[:, sl]
            gh = gate[:, sl]
            o_ref[rows, sl] = o * (gh * _sigmoid(gh))

    nxt = decays(0)
    for sub in range(n_sub):
        cur = nxt
        if sub + 1 < n_sub:
            nxt = decays(sub + 1)
        mix(sub, *cur)


def _hgrn_mix(x, lb_logits, head_norm, layer, n_batch):
    m = x.shape[0]
    n_layers, width = lb_logits.shape
    n_heads = head_norm.shape[0]
    dk = width // n_heads
    rows = HG_BLOCK
    per = m // n_batch // rows
    kern = functools.partial(_hgrn_kernel, layer=layer, n_heads=n_heads, chunk=HG_CHUNK)
    return pl.pallas_call(
        kern,
        grid=(n_batch, per),
        in_specs=[pl.BlockSpec((rows, 4 * width), lambda b, i: (b * per + i, 0)),
                  _const_spec((n_layers, width)),
                  _const_spec((1, width))],
        out_specs=pl.BlockSpec((rows, width), lambda b, i: (b * per + i, 0)),
        out_shape=jax.ShapeDtypeStruct((m, width), F32),
        scratch_shapes=[pltpu.VMEM((n_heads, dk, dk), F32)],
        compiler_params=_cparams("parallel", "arbitrary"),
        name="hgrn_mix",
    )(x, lb_logits, head_norm.reshape(1, width))


_STACK = (0, 2, 1, 3)


def _attn_kernel(sink_ref, q_ref, kv_ref, kvp_ref, o_ref, bias_ref, sinkc_ref, *, n_q, n_kv):
    tq = q_ref.shape[0]
    hd = q_ref.shape[1] // n_q
    grp = n_q // n_kv
    w = WINDOW
    lanes = 2 * hd
    nkw = n_kv * lanes
    first_block = pl.program_id(1) == 0

    @pl.when((pl.program_id(0) == 0) & first_block)
    def _():
        rr = lax.broadcasted_iota(jnp.int32, (2 * w, grp * w), 0)
        cc = lax.broadcasted_iota(jnp.int32, (2 * w, grp * w), 1)
        dist = cc % w + w - rr
        valid = (dist >= 0) & (dist < w)
        distf = dist.astype(F32)
        pos = cc // w
        posr = lax.broadcasted_iota(jnp.int32, (8, grp * w), 1) // w
        for kh in range(n_kv):
            slope = jnp.zeros((2 * w, grp * w), F32)
            sink = jnp.zeros((8, grp * w), F32)
            for i, g in enumerate(_STACK):
                head = kh * grp + g
                slope = jnp.where(pos == i, _LOG2E * 2.0 ** (-8.0 * (head + 1) / n_q), slope)
                sink = jnp.where(posr == i, _LOG2E * sink_ref[head], sink)
            b = jnp.where(valid, -slope * distf, NEG_BIG)
            bias_ref[0, kh] = b
            bias_ref[1, kh] = jnp.where(rr >= w, b, NEG_BIG)
            sinkc_ref[kh] = sink

    lane_q = lax.broadcasted_iota(jnp.int32, (w, lanes), 1)
    lo_q = lane_q < hd
    lane_v = lax.broadcasted_iota(jnp.int32, (2 * w, lanes), 1)
    lo_v = lane_v < hd
    zq = jnp.zeros((w, lanes), BF16)
    zv = jnp.zeros((2 * w, lanes), BF16)
    first_idx = jnp.where(first_block, 1, 0)

    nt = (((1,), (1,)), ((), ()))
    tn = (((0,), (0,)), ((), ()))

    def scores(jb, kh):
        rs = slice(jb * w, (jb + 1) * w)
        ks = slice(kh * lanes, (kh + 1) * lanes)
        if jb == 0:
            kprev = kvp_ref[:, ks]
            bias = bias_ref[first_idx, kh]
        else:
            kprev = kv_ref[(jb - 1) * w:jb * w, ks]
            bias = bias_ref[0, kh]
        kw = jnp.concatenate([kprev, kv_ref[rs, ks]], axis=0)
        qa = q_ref[rs, (2 * kh) * lanes:(2 * kh + 1) * lanes]
        qb = q_ref[rs, (2 * kh + 1) * lanes:(2 * kh + 2) * lanes]
        qs = jnp.concatenate([jnp.where(lo_q, qa, zq), jnp.where(lo_q, qb, zq),
                              jnp.where(lo_q, zq, qa), jnp.where(lo_q, zq, qb)], axis=0)
        return lax.dot_general(kw, qs, nt, preferred_element_type=F32) + bias

    def finish(jb, kh, st):
        rs = slice(jb * w, (jb + 1) * w)
        vs = slice(nkw + kh * lanes, nkw + (kh + 1) * lanes)
        vprev = kvp_ref[:, vs] if jb == 0 else kv_ref[(jb - 1) * w:jb * w, vs]
        vw = jnp.concatenate([vprev, kv_ref[rs, vs]], axis=0)
        sink = sinkc_ref[kh][0:1]
        mx = jnp.maximum(jnp.max(st, axis=0, keepdims=True), sink)
        pe = jnp.exp2(st - mx)
        den = jnp.sum(pe, axis=0, keepdims=True) + jnp.exp2(sink - mx)
        pn = (pe * (1.0 / den)).astype(BF16)
        o = (lax.dot_general(pn[:, 0:2 * w], jnp.where(lo_v, vw, zv), tn, preferred_element_type=F32)
             + lax.dot_general(pn[:, 2 * w:4 * w], jnp.where(lo_v, zv, vw), tn, preferred_element_type=F32))
        o_ref[rs, (2 * kh) * lanes:(2 * kh + 1) * lanes] = o[0:w].astype(o_ref.dtype)
        o_ref[rs, (2 * kh + 1) * lanes:(2 * kh + 2) * lanes] = o[w:2 * w].astype(o_ref.dtype)

    items = [(jb, kh) for jb in range(tq // w) for kh in range(n_kv)]
    pending = [scores(*items[i]) for i in range(ATTN_AHEAD)]
    for idx, item in enumerate(items):
        st = pending.pop(0)
        if idx + ATTN_AHEAD < len(items):
            pending.append(scores(*items[idx + ATTN_AHEAD]))
        finish(*item, st)


def _attention(q, kv, sinks, n_batch):
    m, qw = q.shape
    n_q = sinks.shape[0]
    hd = qw // n_q
    n_kv = kv.shape[1] // (4 * hd)
    grp = n_q // n_kv
    assert grp == 4 and 2 * hd == 128
    tq = ATTN_BLOCK_Q
    per = m // n_batch // tq
    ratio = tq // WINDOW
    kern = functools.partial(_attn_kernel, n_q=n_q, n_kv=n_kv)
    return pl.pallas_call(
        kern,
        grid=(n_batch, per),
        in_specs=[pl.BlockSpec(memory_space=pltpu.SMEM),
                  pl.BlockSpec((tq, qw), lambda b, i: (b * per + i, 0)),
                  pl.BlockSpec((tq, kv.shape[1]), lambda b, i: (b * per + i, 0)),
                  pl.BlockSpec((WINDOW, kv.shape[1]),
                               lambda b, i: (jnp.maximum((b * per + i) * ratio - 1, 0), 0))],
        out_specs=pl.BlockSpec((tq, qw), lambda b, i: (b * per + i, 0)),
        out_shape=jax.ShapeDtypeStruct((m, qw), BF16),
        scratch_shapes=[pltpu.VMEM((2, n_kv, 2 * WINDOW, grp * WINDOW), F32),
                        pltpu.VMEM((n_kv, 8, grp * WINDOW), F32)],
        compiler_params=_cparams("arbitrary", "arbitrary"),
        name="swa_attention",
    )(sinks, q, kv, kv)


def _dup_heads(wcols, n_heads):
    d = wcols.shape[0]
    wh = wcols.reshape(d, n_heads, -1)
    return jnp.concatenate([wh, wh], axis=-1).reshape(d, -1)


def _tail_kernel(*refs, even, final, ff_block):
    refs = list(refs)
    n_out = len(refs) - 1 if even else len(refs)
    o_ref = refs.pop(n_out - 1)
    h_ref = refs.pop(0)
    if even:
        ya_ref, yb_ref, wglu_ref, bglu_ref = refs[:4]
        refs = refs[4:]
    else:
        a_ref = refs.pop(0)
    p_ref, wout_ref, g2_ref, w1_ref, w2_ref, g3_ref, wup_ref, wgate_ref = refs[:8]
    gf_ref = refs[8] if final else None

    tm = h_ref.shape[0]
    halves = [slice(i * tm // TAIL_STREAMS, (i + 1) * tm // TAIL_STREAMS) for i in range(TAIL_STREAMS)]
    hs = [h_ref[r, :] for r in halves]
    if even:
        ya = _from_chunk_rows(ya_ref, refs[-1], S5_CHUNK)
        zs = [jax.nn.gelu(ya[r]) for r in halves]
        oas = [z * _sigmoid(_bdot(z, wglu_ref[...]) + bglu_ref[...]) for z in zs]
        wa = oas[0].shape[-1]
        mixes = [_bdot(oa, wout_ref[0:wa, :]) + _bdot(yb_ref[r, :], wout_ref[wa:, :]) for oa, r in zip(oas, halves)]
    else:
        mixes = [_bdot(a_ref[r, :], wout_ref[...]) for r in halves]
    hs = [h + mix for h, mix in zip(hs, mixes)]

    hns = [_rms(h, g2_ref[...]).astype(BF16) for h in hs]
    d_ff = w1_ref.shape[1]
    accs = [jnp.zeros_like(h) for h in hs]
    for c0 in range(0, d_ff, ff_block):
        for i, hn in enumerate(hns):
            a = jnp.dot(hn, w1_ref[:, c0:c0 + ff_block], preferred_element_type=F32)
            a = jnp.square(jnp.maximum(a, 0.0)).astype(BF16)
            accs[i] = accs[i] + jnp.dot(a, w2_ref[c0:c0 + ff_block, :], preferred_element_type=F32)
    hs = [h + acc for h, acc in zip(hs, accs)]

    hns = [_rms(h, g3_ref[...]) for h in hs]
    gates = [_sigmoid(_bdot(hn, wgate_ref[...])) for hn in hns]
    hs = [h + _bdot(p_ref[r, :], wup_ref[...]) * gate for h, gate, r in zip(hs, gates, halves)]
    for h, r in zip(hs, halves):
        o_ref[r, :] = _rms(h, gf_ref[...]) if final else h


def _tail(h, mixer_ins, mixer_weights, p, layer, w_out, g2, w1, w2, g3, w_up, w_gate, g_final, even):
    m, d = h.shape
    tm = TOKEN_BLOCK
    tok = lambda a: pl.BlockSpec((tm, a.shape[1]), lambda i: (i, 0))
    rowv = lambda a: a.reshape(1, -1)
    weights = [w_out, rowv(g2), w1, w2, rowv(g3), w_up, w_gate]
    final = g_final is not None
    if final:
        weights.append(rowv(g_final))
    args = [h, *mixer_ins, *mixer_weights, p, *weights]
    specs = ([tok(h)] + [tok(a) for a in mixer_ins] + [_const_spec(a.shape) for a in mixer_weights]
             + [pl.BlockSpec((None, tm, p.shape[-1]), lambda i: (layer, i, 0))] + [_const_spec(a.shape) for a in weights])
    scratch = []
    if even:
        ya = mixer_ins[0]
        specs[1] = pl.BlockSpec((tm // S5_CHUNK, ya.shape[1]), lambda i: (i, 0))
        scratch = [pltpu.VMEM((ya.shape[1] // S5_CHUNK // LANES, tm, LANES), F32)]
    kern = functools.partial(_tail_kernel, even=even, final=final, ff_block=1024)
    return pl.pallas_call(
        kern,
        grid=(m // tm,),
        in_specs=specs,
        out_specs=pl.BlockSpec((tm, d), lambda i: (i, 0)),
        out_shape=jax.ShapeDtypeStruct((m, d), F32),
        scratch_shapes=scratch,
        compiler_params=_cparams("parallel"),
        name="layer_tail",
    )(*args)


def kernel(x, p, mix_norm, mlp_norm, ple_norm, final_norm, w_in_even, w_out_even, s5_lam_re, s5_lam_im, s5_log_dt, s5_b_re, s5_b_im, s5_c_re, s5_c_im, s5_d, s5_w_glu, s5_b_glu, hgrn_lb_logits, hgrn_norm, w_qkv_odd, w_o_odd, attn_sinks, w_mlp_in, w_mlp_out, w_ple_up, w_ple_gate):
    bsz, seq, d = x.shape
    depth = p.shape[0]
    m = bsz * seq
    s5_w = s5_w_glu.shape[-1]
    hg_w = hgrn_lb_logits.shape[-1]
    n_q = attn_sinks.shape[-1]
    hd = d // n_q
    kv_w = w_qkv_odd.shape[-1] - n_q * hd
    assert seq % max(TOKEN_BLOCK, ATTN_BLOCK_Q, HG_BLOCK, S5_CHUNK) == 0 and m % PROJ_BLOCK == 0

    h = x.reshape(m, d)
    for i in range(depth):
        j = i // 2
        if i % 2 == 0:
            u, hg = _norm_proj(h, mix_norm[i], w_in_even[j].astype(BF16),
                               [(s5_w, F32, 1.0), (4 * hg_w, F32, 1.0)], chunk_rows=S5_CHUNK)
            ya = _s5_mix(u, s5_lam_re[j], s5_lam_im[j], s5_log_dt[j], s5_b_re[j], s5_b_im[j],
                         s5_c_re[j], s5_c_im[j], s5_d[j], bsz)
            yb = _hgrn_mix(hg, hgrn_lb_logits, hgrn_norm[j], j, bsz)
            mixer_ins = [ya, yb]
            mixer_weights = [s5_w_glu[j].astype(BF16), s5_b_glu[j].reshape(1, -1)]
            w_out = w_out_even[j]
        else:
            wq, wk, wv = jnp.split(w_qkv_odd[j], [n_q * hd, n_q * hd + kv_w // 2], axis=-1)
            n_kv = kv_w // (2 * hd)
            w_qkv = jnp.concatenate([wq, _dup_heads(wk, n_kv), _dup_heads(wv, n_kv)], axis=-1)
            q, kv = _norm_proj(h, mix_norm[i], w_qkv.astype(BF16),
                               [(n_q * hd, BF16, _LOG2E / math.sqrt(hd)), (2 * kv_w, BF16, 1.0)])
            mixer_ins = [_attention(q, kv, attn_sinks[j], bsz)]
            mixer_weights = []
            w_out = w_o_odd[j]
        h = _tail(h, mixer_ins, mixer_weights, p.reshape(depth, m, -1), i, w_out.astype(BF16),
                  mlp_norm[i], w_mlp_in[i].astype(BF16), w_mlp_out[i].astype(BF16),
                  ple_norm[i], w_ple_up[i].astype(BF16), w_ple_gate[i].astype(BF16),
                  final_norm if i == depth - 1 else None, even=(i % 2 == 0))
    return h.reshape(bsz, seq, d)
```

```python
import functools
import math

import jax
import jax.numpy as jnp
from jax import lax
from jax.experimental import pallas as pl
from jax.experimental.pallas import tpu as pltpu

F32 = jnp.float32
BF16 = jnp.bfloat16

EPS = 1e-6
WINDOW = 128
S5_CHUNK = 16
LANES = 128
HG_CHUNK = 128
HG_BLOCK = 1024
LAM_RE_MAX = -1e-4
NEG_BIG = -1e30
_LOG2E = math.log2(math.e)

_V7X_VMEM_BYTES = 64 * 1024 * 1024
_VMEM_LIMIT = _V7X_VMEM_BYTES - 12 * 1024 * 1024

TOKEN_BLOCK = 512
TAIL_STREAMS = 2
PROJ_BLOCK = 1024
ATTN_BLOCK_Q = 1024
ATTN_AHEAD = 2


def _cparams(*sem):
    return pltpu.CompilerParams(dimension_semantics=sem, vmem_limit_bytes=_VMEM_LIMIT)


def _const_spec(shape):
    nd = len(shape)
    return pl.BlockSpec(shape, lambda *_: (0,) * nd, pipeline_mode=pl.Buffered(1))


def _layer_spec(shape, layer):
    nd = len(shape) - 1
    return pl.BlockSpec((None,) + tuple(shape[1:]), lambda *_: (layer,) + (0,) * nd,
                        pipeline_mode=pl.Buffered(1))


def _rms(x, g):
    ms = jnp.mean(x * x, axis=-1, keepdims=True)
    return x * lax.rsqrt(ms + EPS) * g


def _sigmoid(x):
    return 1.0 / (1.0 + jnp.exp(-x))


def _bdot(a, b):
    return jnp.dot(a.astype(BF16), b.astype(BF16), preferred_element_type=F32)


def _to_chunk_rows(val, o_ref, scr_ref, chunk):
    n_tiles = scr_ref.shape[0]
    rows = o_ref.shape[0]
    for k in range(n_tiles):
        scr_ref[k] = val[:, k * LANES:(k + 1) * LANES]
    for k in range(n_tiles):
        for t in range(chunk):
            o_ref[:, (k * chunk + t) * LANES:(k * chunk + t + 1) * LANES] = scr_ref[k, pl.ds(t, rows, stride=chunk), :]


def _from_chunk_rows(x_ref, scr_ref, chunk):
    n_tiles = scr_ref.shape[0]
    rows = x_ref.shape[0]
    for k in range(n_tiles):
        for t in range(chunk):
            scr_ref[k, pl.ds(t, rows, stride=chunk), :] = x_ref[:, (k * chunk + t) * LANES:(k * chunk + t + 1) * LANES]
    return jnp.concatenate([scr_ref[k] for k in range(n_tiles)], axis=1)


def _proj_kernel(h_ref, g_ref, w_ref, *refs, scales, widths, chunk_rows):
    out_refs = refs[:len(scales)]
    hn = _rms(h_ref[...], g_ref[...]).astype(BF16)
    off = 0
    for i, (o_ref, s, n) in enumerate(zip(out_refs, scales, widths)):
        acc = jnp.dot(hn, w_ref[:, off:off + n], preferred_element_type=F32)
        if s != 1.0:
            acc = acc * s
        if i == 0 and chunk_rows:
            _to_chunk_rows(acc, o_ref, refs[-1], chunk_rows)
        else:
            o_ref[...] = acc.astype(o_ref.dtype)
        off += n


def _norm_proj(h, g, w, outs, chunk_rows=0):
    m, d = h.shape
    tm = PROJ_BLOCK
    widths = tuple(n for n, _, _ in outs)
    kern = functools.partial(_proj_kernel, scales=tuple(s for _, _, s in outs), widths=widths,
                             chunk_rows=chunk_rows)
    out_specs = [pl.BlockSpec((tm, n), lambda i: (i, 0)) for n in widths]
    out_shape = [jax.ShapeDtypeStruct((m, n), dt) for n, dt, _ in outs]
    scratch = []
    if chunk_rows:
        n0 = widths[0]
        out_specs[0] = pl.BlockSpec((tm // chunk_rows, n0 * chunk_rows), lambda i: (i, 0))
        out_shape[0] = jax.ShapeDtypeStruct((m // chunk_rows, n0 * chunk_rows), outs[0][1])
        scratch = [pltpu.VMEM((n0 // LANES, tm, LANES), F32)]
    return pl.pallas_call(
        kern,
        grid=(m // tm,),
        in_specs=[pl.BlockSpec((tm, d), lambda i: (i, 0)),
                  _const_spec((1, d)),
                  _const_spec(w.shape)],
        out_specs=out_specs,
        out_shape=out_shape,
        scratch_shapes=scratch,
        compiler_params=_cparams("parallel"),
        name="norm_proj",
    )(h, g.reshape(1, d), w)


def _cmul(ar, ai, br, bi):
    return ar * br - ai * bi, ar * bi + ai * br


def _s5_discretize(lam_re, lam_im, dt):
    lr = jnp.minimum(lam_re, LAM_RE_MAX)
    li = lam_im
    mag = jnp.exp(lr * dt)
    ar = mag * jnp.cos(li * dt)
    ai = mag * jnp.sin(li * dt)
    den = lr * lr + li * li
    xr = ar - 1.0
    zr = (xr * lr + ai * li) / den
    zi = (ai * lr - xr * li) / den
    return ar, ai, zr, zi


def _s5_kernel(u_ref, lam_re_ref, lam_im_ref, logdt_ref, ct_re_ref, ct_im_ref, b_re_ref, b_im_ref,
               lamr_re_ref, lamr_im_ref, logdtr_ref, bt_re_ref, bt_im_ref, d_ref,
               lams_re_ref, lams_im_ref, logdts_ref,
               y_ref, m_ref, qt_ref, pm_ref, *, chunk, n_chan):
    t, hh = chunk, n_chan
    tw = t * LANES
    p = lam_re_ref.shape[1]
    gt = LANES // hh
    sw = gt * p
    nc = u_ref.shape[0]
    n_cbits = max(1, (nc - 1).bit_length())

    @pl.when(pl.program_id(1) == 0)
    def _():
        dt = jnp.exp(logdt_ref[0])
        ar, ai, zr, zi = _s5_discretize(lam_re_ref[0], lam_im_ref[0], dt)
        pw = [(jnp.ones((p, LANES), F32), jnp.zeros((p, LANES), F32))]
        for _ in range(t):
            pw.append(_cmul(pw[-1][0], pw[-1][1], ar, ai))
        bbr, bbi = _cmul(zr, zi, b_re_ref[0], b_im_ref[0])
        ctr, cti = ct_re_ref[0], ct_im_ref[0]
        lane_g = lax.broadcasted_iota(jnp.int32, (p, LANES), 1) // hh
        zero = jnp.zeros((p, LANES), F32)
        q_slots = [_cmul(pw[t - 1 - s][0], pw[t - 1 - s][1], bbr, bbi) for s in range(t)]
        c_slots = [_cmul(ctr, cti, pw[s + 1][0], pw[s + 1][1]) for s in range(t)]
        for g2 in range(gt):
            own = lane_g == g2
            rows_re = slice(g2 * p, (g2 + 1) * p)
            rows_im = slice(sw + g2 * p, sw + (g2 + 1) * p)
            qt_ref[rows_re, :] = jnp.concatenate([jnp.where(own, r, zero) for r, _ in q_slots], axis=1).astype(BF16)
            qt_ref[rows_im, :] = jnp.concatenate([jnp.where(own, i, zero) for _, i in q_slots], axis=1).astype(BF16)
            pm_ref[rows_re, :] = jnp.concatenate([jnp.where(own, r, zero) for r, _ in c_slots], axis=1).astype(BF16)
            pm_ref[rows_im, :] = jnp.concatenate([jnp.where(own, -i, zero) for _, i in c_slots], axis=1).astype(BF16)

        _, _, zr2, zi2 = _s5_discretize(lamr_re_ref[0], lamr_im_ref[0], jnp.exp(logdtr_ref[0]))
        first = lax.broadcasted_iota(jnp.int32, (LANES, 2 * p), 1) < p
        btr, bti = bt_re_ref[0], bt_im_ref[0]
        lhs = jnp.where(first, zr2 * btr - zi2 * bti, -(zr2 * bti + zi2 * btr))
        rhs = jnp.concatenate(
            [jnp.concatenate([_cmul(ctr, cti, pw[tau][0], pw[tau][1])[0] for tau in range(t)], axis=1),
             jnp.concatenate([_cmul(ctr, cti, pw[tau][0], pw[tau][1])[1] for tau in range(t)], axis=1)],
            axis=0)
        lh, ll = lhs.astype(BF16), (lhs - lhs.astype(BF16).astype(F32)).astype(BF16)
        rh, rl = rhs.astype(BF16), (rhs - rhs.astype(BF16).astype(F32)).astype(BF16)
        z = (jnp.dot(lh, rh, preferred_element_type=F32)
             + jnp.dot(lh, rl, preferred_element_type=F32)
             + jnp.dot(ll, rh, preferred_element_type=F32))
        row_z = lax.broadcasted_iota(jnp.int32, (LANES, tw), 0)
        lane_z = lax.broadcasted_iota(jnp.int32, (LANES, tw), 1)
        z = jnp.where(row_z // hh == (lane_z % LANES) // hh, z, 0.0)
        d_diag = jnp.where(lax.broadcasted_iota(jnp.int32, (LANES, LANES), 0)
                           == lax.broadcasted_iota(jnp.int32, (LANES, LANES), 1), d_ref[0], 0.0)
        z = jnp.concatenate([z[:, 0:LANES] + d_diag, z[:, LANES:]], axis=1)
        zb = z.astype(BF16)
        for s_in in range(t):
            if s_in:
                m_ref[s_in * LANES:(s_in + 1) * LANES, 0:s_in * LANES] = jnp.zeros((LANES, s_in * LANES), BF16)
            m_ref[s_in * LANES:(s_in + 1) * LANES, s_in * LANES:tw] = zb[:, 0:tw - s_in * LANES]

    u = u_ref[...].astype(BF16)
    y = jnp.concatenate(
        [jnp.dot(u[:, 0:(i + 2) * LANES], m_ref[0:(i + 2) * LANES, i * LANES:(i + 2) * LANES],
                 preferred_element_type=F32) for i in range(0, t, 2)], axis=1)
    x = lax.dot_general(u, qt_ref[...], (((1,), (1,)), ((), ())), preferred_element_type=F32)

    mr, mi, _, _ = _s5_discretize(lams_re_ref[0], lams_im_ref[0], jnp.exp(logdts_ref[0]))
    for _ in range(t.bit_length() - 1):
        mr, mi = _cmul(mr, mi, mr, mi)
    ridx = lax.broadcasted_iota(jnp.int32, (nc, sw), 0)

    def shift_down(v, k):
        if k % 8 == 0:
            return jnp.concatenate([jnp.zeros((k, sw), F32), v[:nc - k]], axis=0)
        return jnp.where(ridx >= k, pltpu.roll(v, k, axis=0), 0.0)

    s_re, s_im = x[:, 0:sw], x[:, sw:2 * sw]
    for k in range(n_cbits):
        h_re, h_im = shift_down(s_re, 1 << k), shift_down(s_im, 1 << k)
        s_re, s_im = s_re + (mr * h_re - mi * h_im), s_im + (mr * h_im + mi * h_re)
        mr, mi = _cmul(mr, mi, mr, mi)
    s_prev = jnp.concatenate([shift_down(s_re, 1), shift_down(s_im, 1)], axis=1).astype(BF16)
    y_ref[...] = y + jnp.dot(s_prev, pm_ref[...], preferred_element_type=F32)


def _s5_mix(u, lam_re, lam_im, log_dt, b_re, b_im, c_re, c_im, d_skip, n_batch):
    rows, _ = u.shape
    g, p, hh = b_re.shape
    t = S5_CHUNK
    gt = LANES // hh
    nt = g // gt
    tw = t * LANES
    nc = rows // n_batch

    def lanes_gh(a):
        return a.reshape(nt, gt, p, hh).transpose(0, 2, 1, 3).reshape(nt, p, LANES)

    def rows_gh(a):
        r = a.reshape(nt, gt, p, hh).transpose(0, 1, 3, 2).reshape(nt, LANES, p)
        return jnp.concatenate([r, r], axis=-1)

    per_gh = lambda a: jnp.broadcast_to(a.reshape(g, p, 1), (g, p, hh))
    lam_re_c, lam_im_c = lanes_gh(per_gh(lam_re)), lanes_gh(per_gh(lam_im))
    logdt_c = jnp.broadcast_to(log_dt.reshape(nt, gt, 1), (nt, gt, hh)).reshape(nt, 1, LANES)
    ct_re, ct_im = lanes_gh(c_re.transpose(0, 2, 1)), lanes_gh(c_im.transpose(0, 2, 1))
    b_re_c, b_im_c = lanes_gh(b_re), lanes_gh(b_im)
    lamr = lambda a: rows_gh(jnp.broadcast_to(a.reshape(g, p, 1), (g, p, hh)))
    logdt_r = jnp.broadcast_to(log_dt.reshape(nt, gt, 1), (nt, gt, hh)).reshape(nt, LANES, 1)
    d_c = d_skip.reshape(nt, 1, LANES)
    lams = lambda a: a.reshape(nt, 1, gt * p)
    logdt_s = jnp.broadcast_to(log_dt.reshape(nt, gt, 1), (nt, gt, p)).reshape(nt, 1, gt * p)

    params = [lam_re_c, lam_im_c, logdt_c, ct_re, ct_im, b_re_c, b_im_c,
              lamr(lam_re), lamr(lam_im), logdt_r, rows_gh(b_re), rows_gh(b_im), d_c,
              lams(lam_re), lams(lam_im), logdt_s]
    pspec = lambda a: pl.BlockSpec((1,) + a.shape[1:], lambda k, b: (k,) + (0,) * (a.ndim - 1))
    kern = functools.partial(_s5_kernel, chunk=t, n_chan=hh)
    return pl.pallas_call(
        kern,
        grid=(nt, n_batch),
        in_specs=[pl.BlockSpec((nc, tw), lambda k, b: (b, k))] + [pspec(a) for a in params],
        out_specs=pl.BlockSpec((nc, tw), lambda k, b: (b, k)),
        out_shape=jax.ShapeDtypeStruct(u.shape, F32),
        scratch_shapes=[pltpu.VMEM((tw, tw), BF16),
                        pltpu.VMEM((2 * gt * p, tw), BF16),
                        pltpu.VMEM((2 * gt * p, tw), BF16)],
        compiler_params=_cparams("arbitrary", "arbitrary"),
        name="s5_mix",
    )(u, *params)


HG_LOW_LEVELS = 3


def _hgrn_kernel(x_ref, lbl_ref, hn_ref, o_ref, st_ref, *, layer, n_heads, chunk):
    c = chunk
    n_sub = x_ref.shape[0] // c
    width = o_ref.shape[-1]
    dk = width // n_heads
    n_lev = c.bit_length() - 1

    @pl.when(pl.program_id(1) == 0)
    def _():
        st_ref[...] = jnp.zeros_like(st_ref)

    lg = lbl_ref[...]
    e = jnp.exp(lg - jnp.max(lg, axis=0, keepdims=True))
    sm = e / jnp.sum(e, axis=0, keepdims=True)
    lb = jnp.zeros((1, width), F32)
    for r in range(1, layer + 1):
        lb = lb + sm[r:r + 1]
    lb_floor = jnp.maximum(lb, 1e-30)
    one_m_lb = 1.0 - lb
    hnw = hn_ref[...]

    ri = lax.broadcasted_iota(jnp.int32, (c, c), 0)
    ci = lax.broadcasted_iota(jnp.int32, (c, c), 1)
    tri = jnp.where(ci <= ri, 1.0, 0.0).astype(BF16)
    keeps, lows = [], []
    for l in range(1, n_lev + 1):
        blk, half = 1 << l, 1 << (l - 1)
        keeps.append((ri // blk == ci // blk) & (ri % blk >= half) & (ci % blk < half))
        if l <= HG_LOW_LEVELS:
            bd = (ri // blk) * blk + half - 1
            lows.append(jnp.where(((ci > bd) & (ci <= ri)) | ((ci > ri) & (ci <= bd)), 1.0, 0.0))
    w_low = jnp.concatenate(lows, axis=0).astype(BF16)
    diag = ri == ci

    nt = (((1,), (1,)), ((), ()))
    tn = (((0,), (0,)), ((), ()))
    def decays(sub):
        rows = slice(sub * c, (sub + 1) * c)
        q = x_ref[rows, 0:width]
        zf = x_ref[rows, width:2 * width]
        sig = 1.0 / (1.0 + jnp.exp(-zf))
        logf = jnp.log(lb_floor + one_m_lb * sig)
        k = one_m_lb * (1.0 - sig)
        hi = logf.astype(BF16)
        mid = (logf - hi.astype(F32)).astype(BF16)
        b = jnp.dot(tri, hi, preferred_element_type=F32) + jnp.dot(tri, mid, preferred_element_type=F32)
        ex_low = (jnp.dot(w_low, hi, preferred_element_type=F32)
                  + jnp.dot(w_low, mid, preferred_element_type=F32))
        return q, k, b, ex_low

    def mix(sub, q, k, b, ex_low):
        rows = slice(sub * c, (sub + 1) * c)
        v = x_ref[rows, 2 * width:3 * width]
        gate = x_ref[rows, 3 * width:4 * width]
        qb, kb = q.astype(BF16), k.astype(BF16)
        att = [jnp.where(diag, jnp.sum((q * k)[:, h * dk:(h + 1) * dk], axis=-1, keepdims=True), 0.0)
               for h in range(n_heads)]
        for l in range(1, n_lev + 1):
            blk, half = 1 << l, 1 << (l - 1)
            if l <= HG_LOW_LEVELS:
                ex = ex_low[(l - 1) * c:l * c]
            else:
                pieces = []
                for s0 in range(0, c, blk):
                    bd = b[s0 + half - 1:s0 + half]
                    pieces.append(bd - b[s0:s0 + half])
                    pieces.append(b[s0 + half:s0 + blk] - bd)
                ex = jnp.concatenate(pieces, axis=0)
            wgt = jnp.exp(ex).astype(BF16)
            ql, kl = qb * wgt, kb * wgt
            for h in range(n_heads):
                sl = slice(h * dk, (h + 1) * dk)
                pm = lax.dot_general(ql[:, sl], kl[:, sl], nt, preferred_element_type=F32)
                att[h] = att[h] + jnp.where(keeps[l - 1], pm, 0.0)

        b_last = b[c - 1:c]
        q_in = (q * jnp.exp(b)).astype(BF16)
        k_out = (k * jnp.exp(b_last - b)).astype(BF16)
        e_last = jnp.exp(b_last)
        vb = v.astype(BF16)
        for h in range(n_heads):
            sl = slice(h * dk, (h + 1) * dk)
            st = st_ref[h]
            o = lax.dot_general(q_in[:, sl], st.astype(BF16), nt, preferred_element_type=F32)
            o = o + jnp.dot(att[h].astype(BF16), vb[:, sl], preferred_element_type=F32)
            st_ref[h] = st * e_last[:, sl] + lax.dot_general(vb[:, sl], k_out[:, sl], tn,
                                                              preferred_element_type=F32)
            o = o * lax.rsqrt(jnp.mean(o * o, axis=-1, keepdims=True) + EPS) * hnw[:, sl]
            gh = gate[:, sl]
            o_ref[rows, sl] = o * (gh * _sigmoid(gh))

    nxt = decays(0)
    for sub in range(n_sub):
        cur = nxt
        if sub + 1 < n_sub:
            nxt = decays(sub + 1)
        mix(sub, *cur)


def _hgrn_mix(x, lb_logits, head_norm, layer, n_batch):
    m = x.shape[0]
    n_layers, width = lb_logits.shape
    n_heads = head_norm.shape[0]
    dk = width // n_heads
    rows = HG_BLOCK
    per = m // n_batch // rows
    kern = functools.partial(_hgrn_kernel, layer=layer, n_heads=n_heads, chunk=HG_CHUNK)
    return pl.pallas_call(
        kern,
        grid=(n_batch, per),
        in_specs=[pl.BlockSpec((rows, 4 * width), lambda b, i: (b * per + i, 0)),
                  _const_spec((n_layers, width)),
                  _const_spec((1, width))],
        out_specs=pl.BlockSpec((rows, width), lambda b, i: (b * per + i, 0)),
        out_shape=jax.ShapeDtypeStruct((m, width), F32),
        scratch_shapes=[pltpu.VMEM((n_heads, dk, dk), F32)],
        compiler_params=_cparams("parallel", "arbitrary"),
        name="hgrn_mix",
    )(x, lb_logits, head_norm.reshape(1, width))


_STACK = (0, 2, 1, 3)


def _attn_kernel(sink_ref, q_ref, kv_ref, kvp_ref, o_ref, bias_ref, sinkc_ref, *, n_q, n_kv):
    tq = q_ref.shape[0]
    hd = q_ref.shape[1] // n_q
    grp = n_q // n_kv
    w = WINDOW
    lanes = 2 * hd
    nkw = n_kv * lanes
    first_block = pl.program_id(1) == 0

    @pl.when((pl.program_id(0) == 0) & first_block)
    def _():
        rr = lax.broadcasted_iota(jnp.int32, (2 * w, grp * w), 0)
        cc = lax.broadcasted_iota(jnp.int32, (2 * w, grp * w), 1)
        dist = cc % w + w - rr
        valid = (dist >= 0) & (dist < w)
        distf = dist.astype(F32)
        pos = cc // w
        posr = lax.broadcasted_iota(jnp.int32, (8, grp * w), 1) // w
        for kh in range(n_kv):
            slope = jnp.zeros((2 * w, grp * w), F32)
            sink = jnp.zeros((8, grp * w), F32)
            for i, g in enumerate(_STACK):
                head = kh * grp + g
                slope = jnp.where(pos == i, _LOG2E * 2.0 ** (-8.0 * (head + 1) / n_q), slope)
                sink = jnp.where(posr == i, _LOG2E * sink_ref[head], sink)
            b = jnp.where(valid, -slope * distf, NEG_BIG)
            bias_ref[0, kh] = b
            bias_ref[1, kh] = jnp.where(rr >= w, b, NEG_BIG)
            sinkc_ref[kh] = sink

    lane_q = lax.broadcasted_iota(jnp.int32, (w, lanes), 1)
    lo_q = lane_q < hd
    lane_v = lax.broadcasted_iota(jnp.int32, (2 * w, lanes), 1)
    lo_v = lane_v < hd
    zq = jnp.zeros((w, lanes), BF16)
    zv = jnp.zeros((2 * w, lanes), BF16)
    first_idx = jnp.where(first_block, 1, 0)

    nt = (((1,), (1,)), ((), ()))
    tn = (((0,), (0,)), ((), ()))

    def scores(jb, kh):
        rs = slice(jb * w, (jb + 1) * w)
        ks = slice(kh * lanes, (kh + 1) * lanes)
        if jb == 0:
            kprev = kvp_ref[:, ks]
            bias = bias_ref[first_idx, kh]
        else:
            kprev = kv_ref[(jb - 1) * w:jb * w, ks]
            bias = bias_ref[0, kh]
        kw = jnp.concatenate([kprev, kv_ref[rs, ks]], axis=0)
        qa = q_ref[rs, (2 * kh) * lanes:(2 * kh + 1) * lanes]
        qb = q_ref[rs, (2 * kh + 1) * lanes:(2 * kh + 2) * lanes]
        qs = jnp.concatenate([jnp.where(lo_q, qa, zq), jnp.where(lo_q, qb, zq),
                              jnp.where(lo_q, zq, qa), jnp.where(lo_q, zq, qb)], axis=0)
        return lax.dot_general(kw, qs, nt, preferred_element_type=F32) + bias

    def finish(jb, kh, st):
        rs = slice(jb * w, (jb + 1) * w)
        vs = slice(nkw + kh * lanes, nkw + (kh + 1) * lanes)
        vprev = kvp_ref[:, vs] if jb == 0 else kv_ref[(jb - 1) * w:jb * w, vs]
        vw = jnp.concatenate([vprev, kv_ref[rs, vs]], axis=0)
        sink = sinkc_ref[kh][0:1]
        mx = jnp.maximum(jnp.max(st, axis=0, keepdims=True), sink)
        pe = jnp.exp2(st - mx)
        den = jnp.sum(pe, axis=0, keepdims=True) + jnp.exp2(sink - mx)
        pn = (pe * (1.0 / den)).astype(BF16)
        o = (lax.dot_general(pn[:, 0:2 * w], jnp.where(lo_v, vw, zv), tn, preferred_element_type=F32)
             + lax.dot_general(pn[:, 2 * w:4 * w], jnp.where(lo_v, zv, vw), tn, preferred_element_type=F32))
        o_ref[rs, (2 * kh) * lanes:(2 * kh + 1) * lanes] = o[0:w].astype(o_ref.dtype)
        o_ref[rs, (2 * kh + 1) * lanes:(2 * kh + 2) * lanes] = o[w:2 * w].astype(o_ref.dtype)

    items = [(jb, kh) for jb in range(tq // w) for kh in range(n_kv)]
    pending = [scores(*items[i]) for i in range(ATTN_AHEAD)]
    for idx, item in enumerate(items):
        st = pending.pop(0)
        if idx + ATTN_AHEAD < len(items):
            pending.append(scores(*items[idx + ATTN_AHEAD]))
        finish(*item, st)


def _attention(q, kv, sinks, n_batch):
    m, qw = q.shape
    n_q = sinks.shape[0]
    hd = qw // n_q
    n_kv = kv.shape[1] // (4 * hd)
    grp = n_q // n_kv
    assert grp == 4 and 2 * hd == 128
    tq = ATTN_BLOCK_Q
    per = m // n_batch // tq
    ratio = tq // WINDOW
    kern = functools.partial(_attn_kernel, n_q=n_q, n_kv=n_kv)
    return pl.pallas_call(
        kern,
        grid=(n_batch, per),
        in_specs=[pl.BlockSpec(memory_space=pltpu.SMEM),
                  pl.BlockSpec((tq, qw), lambda b, i: (b * per + i, 0)),
                  pl.BlockSpec((tq, kv.shape[1]), lambda b, i: (b * per + i, 0)),
                  pl.BlockSpec((WINDOW, kv.shape[1]),
                               lambda b, i: (jnp.maximum((b * per + i) * ratio - 1, 0), 0))],
        out_specs=pl.BlockSpec((tq, qw), lambda b, i: (b * per + i, 0)),
        out_shape=jax.ShapeDtypeStruct((m, qw), BF16),
        scratch_shapes=[pltpu.VMEM((2, n_kv, 2 * WINDOW, grp * WINDOW), F32),
                        pltpu.VMEM((n_kv, 8, grp * WINDOW), F32)],
        compiler_params=_cparams("arbitrary", "arbitrary"),
        name="swa_attention",
    )(sinks, q, kv, kv)


def _dup_heads(wcols, n_heads):
    d = wcols.shape[0]
    wh = wcols.reshape(d, n_heads, -1)
    return jnp.concatenate([wh, wh], axis=-1).reshape(d, -1)


def _tail_kernel(*refs, even, final, ff_block):
    refs = list(refs)
    n_out = len(refs) - 1 if even else len(refs)
    o_ref = refs.pop(n_out - 1)
    h_ref = refs.pop(0)
    if even:
        ya_ref, yb_ref, wglu_ref, bglu_ref = refs[:4]
        refs = refs[4:]
    else:
        a_ref = refs.pop(0)
    p_ref, wout_ref, g2_ref, w1_ref, w2_ref, g3_ref, wup_ref, wgate_ref = refs[:8]
    gf_ref = refs[8] if final else None

    tm = h_ref.shape[0]
    halves = [slice(i * tm // TAIL_STREAMS, (i + 1) * tm // TAIL_STREAMS) for i in range(TAIL_STREAMS)]
    hs = [h_ref[r, :] for r in halves]
    if even:
        ya = _from_chunk_rows(ya_ref, refs[-1], S5_CHUNK)
        zs = [jax.nn.gelu(ya[r]) for r in halves]
        oas = [z * _sigmoid(_bdot(z, wglu_ref[...]) + bglu_ref[...]) for z in zs]
        wa = oas[0].shape[-1]
        mixes = [_bdot(oa, wout_ref[0:wa, :]) + _bdot(yb_ref[r, :], wout_ref[wa:, :]) for oa, r in zip(oas, halves)]
    else:
        mixes = [_bdot(a_ref[r, :], wout_ref[...]) for r in halves]
    hs = [h + mix for h, mix in zip(hs, mixes)]

    hns = [_rms(h, g2_ref[...]).astype(BF16) for h in hs]
    d_ff = w1_ref.shape[1]
    accs = [jnp.zeros_like(h) for h in hs]
    for c0 in range(0, d_ff, ff_block):
        for i, hn in enumerate(hns):
            a = jnp.dot(hn, w1_ref[:, c0:c0 + ff_block], preferred_element_type=F32)
            a = jnp.square(jnp.maximum(a, 0.0)).astype(BF16)
            accs[i] = accs[i] + jnp.dot(a, w2_ref[c0:c0 + ff_block, :], preferred_element_type=F32)
    hs = [h + acc for h, acc in zip(hs, accs)]

    hns = [_rms(h, g3_ref[...]) for h in hs]
    gates = [_sigmoid(_bdot(hn, wgate_ref[...])) for hn in hns]
    hs = [h + _bdot(p_ref[r, :], wup_ref[...]) * gate for h, gate, r in zip(hs, gates, halves)]
    for h, r in zip(hs, halves):
        o_ref[r, :] = _rms(h, gf_ref[...]) if final else h


def _tail(h, mixer_ins, mixer_weights, p, layer, w_out, g2, w1, w2, g3, w_up, w_gate, g_final, even):
    m, d = h.shape
    tm = TOKEN_BLOCK
    tok = lambda a: pl.BlockSpec((tm, a.shape[1]), lambda i: (i, 0))
    rowv = lambda a: a.reshape(1, -1)
    weights = [w_out, rowv(g2), w1, w2, rowv(g3), w_up, w_gate]
    stacked = (2, 3, 5, 6)
    final = g_final is not None
    if final:
        weights.append(rowv(g_final))
    args = [h, *mixer_ins, *mixer_weights, p, *weights]
    specs = ([tok(h)] + [tok(a) for a in mixer_ins] + [_const_spec(a.shape) for a in mixer_weights]
             + [pl.BlockSpec((None, tm, p.shape[-1]), lambda i: (layer, i, 0))]
             + [_layer_spec(a.shape, layer) if k in stacked else _const_spec(a.shape) for k, a in enumerate(weights)])
    scratch = []
    if even:
        ya = mixer_ins[0]
        specs[1] = pl.BlockSpec((tm // S5_CHUNK, ya.shape[1]), lambda i: (i, 0))
        scratch = [pltpu.VMEM((ya.shape[1] // S5_CHUNK // LANES, tm, LANES), F32)]
    kern = functools.partial(_tail_kernel, even=even, final=final, ff_block=1024)
    return pl.pallas_call(
        kern,
        grid=(m // tm,),
        in_specs=specs,
        out_specs=pl.BlockSpec((tm, d), lambda i: (i, 0)),
        out_shape=jax.ShapeDtypeStruct((m, d), F32),
        scratch_shapes=scratch,
        compiler_params=_cparams("parallel"),
        name="layer_tail",
    )(*args)


def kernel(x, p, mix_norm, mlp_norm, ple_norm, final_norm, w_in_even, w_out_even, s5_lam_re, s5_lam_im, s5_log_dt, s5_b_re, s5_b_im, s5_c_re, s5_c_im, s5_d, s5_w_glu, s5_b_glu, hgrn_lb_logits, hgrn_norm, w_qkv_odd, w_o_odd, attn_sinks, w_mlp_in, w_mlp_out, w_ple_up, w_ple_gate):
    bsz, seq, d = x.shape
    depth = p.shape[0]
    m = bsz * seq
    s5_w = s5_w_glu.shape[-1]
    hg_w = hgrn_lb_logits.shape[-1]
    n_q = attn_sinks.shape[-1]
    hd = d // n_q
    kv_w = w_qkv_odd.shape[-1] - n_q * hd
    assert seq % max(TOKEN_BLOCK, ATTN_BLOCK_Q, HG_BLOCK, S5_CHUNK) == 0 and m % PROJ_BLOCK == 0

    w1_all, w2_all = w_mlp_in.astype(BF16), w_mlp_out.astype(BF16)
    w_up_all, w_gate_all = w_ple_up.astype(BF16), w_ple_gate.astype(BF16)
    h = x.reshape(m, d)
    for i in range(depth):
        j = i // 2
        if i % 2 == 0:
            u, hg = _norm_proj(h, mix_norm[i], w_in_even[j].astype(BF16),
                               [(s5_w, F32, 1.0), (4 * hg_w, F32, 1.0)], chunk_rows=S5_CHUNK)
            ya = _s5_mix(u, s5_lam_re[j], s5_lam_im[j], s5_log_dt[j], s5_b_re[j], s5_b_im[j],
                         s5_c_re[j], s5_c_im[j], s5_d[j], bsz)
            yb = _hgrn_mix(hg, hgrn_lb_logits, hgrn_norm[j], j, bsz)
            mixer_ins = [ya, yb]
            mixer_weights = [s5_w_glu[j].astype(BF16), s5_b_glu[j].reshape(1, -1)]
            w_out = w_out_even[j]
        else:
            wq, wk, wv = jnp.split(w_qkv_odd[j], [n_q * hd, n_q * hd + kv_w // 2], axis=-1)
            n_kv = kv_w // (2 * hd)
            w_qkv = jnp.concatenate([wq, _dup_heads(wk, n_kv), _dup_heads(wv, n_kv)], axis=-1)
            q, kv = _norm_proj(h, mix_norm[i], w_qkv.astype(BF16),
                               [(n_q * hd, BF16, _LOG2E / math.sqrt(hd)), (2 * kv_w, BF16, 1.0)])
            mixer_ins = [_attention(q, kv, attn_sinks[j], bsz)]
            mixer_weights = []
            w_out = w_o_odd[j]
        h = _tail(h, mixer_ins, mixer_weights, p.reshape(depth, m, -1), i, w_out.astype(BF16),
                  mlp_norm[i], w1_all, w2_all, ple_norm[i], w_up_all, w_gate_all,
                  final_norm if i == depth - 1 else None, even=(i % 2 == 0))
    return h.reshape(bsz, seq, d)
```

```python
import functools
import math

import jax
import jax.numpy as jnp
from jax import lax
from jax.experimental import pallas as pl
from jax.experimental.pallas import tpu as pltpu

F32 = jnp.float32
BF16 = jnp.bfloat16

EPS = 1e-6
WINDOW = 128
S5_CHUNK = 16
LANES = 128
HG_CHUNK = 128
HG_BLOCK = 1024
LAM_RE_MAX = -1e-4
NEG_BIG = -1e30
_LOG2E = math.log2(math.e)

_V7X_VMEM_BYTES = 64 * 1024 * 1024
_VMEM_LIMIT = _V7X_VMEM_BYTES - 12 * 1024 * 1024

TOKEN_BLOCK = 512
TAIL_STREAMS = 2
PROJ_BLOCK = 1024
ATTN_BLOCK_Q = 2048
ATTN_AHEAD = 3


def _cparams(*sem):
    return pltpu.CompilerParams(dimension_semantics=sem, vmem_limit_bytes=_VMEM_LIMIT)


def _const_spec(shape):
    nd = len(shape)
    return pl.BlockSpec(shape, lambda *_: (0,) * nd, pipeline_mode=pl.Buffered(1))


def _layer_spec(shape, layer):
    nd = len(shape) - 1
    return pl.BlockSpec((None,) + tuple(shape[1:]), lambda *_: (layer,) + (0,) * nd,
                        pipeline_mode=pl.Buffered(1))


def _rms(x, g):
    ms = jnp.mean(x * x, axis=-1, keepdims=True)
    return x * lax.rsqrt(ms + EPS) * g


def _sigmoid(x):
    return 1.0 / (1.0 + jnp.exp(-x))


def _bdot(a, b):
    return jnp.dot(a.astype(BF16), b.astype(BF16), preferred_element_type=F32)


def _to_chunk_rows(val, o_ref, scr_ref, chunk):
    n_tiles = scr_ref.shape[0]
    rows = o_ref.shape[0]
    for k in range(n_tiles):
        scr_ref[k] = val[:, k * LANES:(k + 1) * LANES]
    for k in range(n_tiles):
        for t in range(chunk):
            o_ref[:, (k * chunk + t) * LANES:(k * chunk + t + 1) * LANES] = scr_ref[k, pl.ds(t, rows, stride=chunk), :]


def _from_chunk_rows(x_ref, scr_ref, chunk):
    n_tiles = scr_ref.shape[0]
    rows = x_ref.shape[0]
    for k in range(n_tiles):
        for t in range(chunk):
            scr_ref[k, pl.ds(t, rows, stride=chunk), :] = x_ref[:, (k * chunk + t) * LANES:(k * chunk + t + 1) * LANES]
    return jnp.concatenate([scr_ref[k] for k in range(n_tiles)], axis=1)


def _proj_kernel(h_ref, g_ref, w_ref, *refs, scales, widths, chunk_rows):
    out_refs = refs[:len(scales)]
    hn = _rms(h_ref[...], g_ref[...]).astype(BF16)
    off = 0
    for i, (o_ref, s, n) in enumerate(zip(out_refs, scales, widths)):
        acc = jnp.dot(hn, w_ref[:, off:off + n], preferred_element_type=F32)
        if s != 1.0:
            acc = acc * s
        if i == 0 and chunk_rows:
            _to_chunk_rows(acc, o_ref, refs[-1], chunk_rows)
        else:
            o_ref[...] = acc.astype(o_ref.dtype)
        off += n


def _norm_proj(h, g, w, outs, chunk_rows=0):
    m, d = h.shape
    tm = PROJ_BLOCK
    widths = tuple(n for n, _, _ in outs)
    kern = functools.partial(_proj_kernel, scales=tuple(s for _, _, s in outs), widths=widths,
                             chunk_rows=chunk_rows)
    out_specs = [pl.BlockSpec((tm, n), lambda i: (i, 0)) for n in widths]
    out_shape = [jax.ShapeDtypeStruct((m, n), dt) for n, dt, _ in outs]
    scratch = []
    if chunk_rows:
        n0 = widths[0]
        out_specs[0] = pl.BlockSpec((tm // chunk_rows, n0 * chunk_rows), lambda i: (i, 0))
        out_shape[0] = jax.ShapeDtypeStruct((m // chunk_rows, n0 * chunk_rows), outs[0][1])
        scratch = [pltpu.VMEM((n0 // LANES, tm, LANES), F32)]
    return pl.pallas_call(
        kern,
        grid=(m // tm,),
        in_specs=[pl.BlockSpec((tm, d), lambda i: (i, 0)),
                  _const_spec((1, d)),
                  _const_spec(w.shape)],
        out_specs=out_specs,
        out_shape=out_shape,
        scratch_shapes=scratch,
        compiler_params=_cparams("parallel"),
        name="norm_proj",
    )(h, g.reshape(1, d), w)


def _cmul(ar, ai, br, bi):
    return ar * br - ai * bi, ar * bi + ai * br


def _s5_discretize(lam_re, lam_im, dt):
    lr = jnp.minimum(lam_re, LAM_RE_MAX)
    li = lam_im
    mag = jnp.exp(lr * dt)
    ar = mag * jnp.cos(li * dt)
    ai = mag * jnp.sin(li * dt)
    den = lr * lr + li * li
    xr = ar - 1.0
    zr = (xr * lr + ai * li) / den
    zi = (ai * lr - xr * li) / den
    return ar, ai, zr, zi


def _s5_kernel(u_ref, lam_re_ref, lam_im_ref, logdt_ref, ct_re_ref, ct_im_ref, b_re_ref, b_im_ref,
               lamr_re_ref, lamr_im_ref, logdtr_ref, bt_re_ref, bt_im_ref, d_ref,
               lams_re_ref, lams_im_ref, logdts_ref,
               y_ref, m_ref, qt_ref, pm_ref, *, chunk, n_chan):
    t, hh = chunk, n_chan
    tw = t * LANES
    p = lam_re_ref.shape[1]
    gt = LANES // hh
    sw = gt * p
    nc = u_ref.shape[0]
    n_cbits = max(1, (nc - 1).bit_length())

    @pl.when(pl.program_id(1) == 0)
    def _():
        dt = jnp.exp(logdt_ref[0])
        ar, ai, zr, zi = _s5_discretize(lam_re_ref[0], lam_im_ref[0], dt)
        pw = [(jnp.ones((p, LANES), F32), jnp.zeros((p, LANES), F32))]
        for _ in range(t):
            pw.append(_cmul(pw[-1][0], pw[-1][1], ar, ai))
        bbr, bbi = _cmul(zr, zi, b_re_ref[0], b_im_ref[0])
        ctr, cti = ct_re_ref[0], ct_im_ref[0]
        lane_g = lax.broadcasted_iota(jnp.int32, (p, LANES), 1) // hh
        zero = jnp.zeros((p, LANES), F32)
        q_slots = [_cmul(pw[t - 1 - s][0], pw[t - 1 - s][1], bbr, bbi) for s in range(t)]
        c_slots = [_cmul(ctr, cti, pw[s + 1][0], pw[s + 1][1]) for s in range(t)]
        for g2 in range(gt):
            own = lane_g == g2
            rows_re = slice(g2 * p, (g2 + 1) * p)
            rows_im = slice(sw + g2 * p, sw + (g2 + 1) * p)
            qt_ref[rows_re, :] = jnp.concatenate([jnp.where(own, r, zero) for r, _ in q_slots], axis=1).astype(BF16)
            qt_ref[rows_im, :] = jnp.concatenate([jnp.where(own, i, zero) for _, i in q_slots], axis=1).astype(BF16)
            pm_ref[rows_re, :] = jnp.concatenate([jnp.where(own, r, zero) for r, _ in c_slots], axis=1).astype(BF16)
            pm_ref[rows_im, :] = jnp.concatenate([jnp.where(own, -i, zero) for _, i in c_slots], axis=1).astype(BF16)

        _, _, zr2, zi2 = _s5_discretize(lamr_re_ref[0], lamr_im_ref[0], jnp.exp(logdtr_ref[0]))
        first = lax.broadcasted_iota(jnp.int32, (LANES, 2 * p), 1) < p
        btr, bti = bt_re_ref[0], bt_im_ref[0]
        lhs = jnp.where(first, zr2 * btr - zi2 * bti, -(zr2 * bti + zi2 * btr))
        rhs = jnp.concatenate(
            [jnp.concatenate([_cmul(ctr, cti, pw[tau][0], pw[tau][1])[0] for tau in range(t)], axis=1),
             jnp.concatenate([_cmul(ctr, cti, pw[tau][0], pw[tau][1])[1] for tau in range(t)], axis=1)],
            axis=0)
        lh, ll = lhs.astype(BF16), (lhs - lhs.astype(BF16).astype(F32)).astype(BF16)
        rh, rl = rhs.astype(BF16), (rhs - rhs.astype(BF16).astype(F32)).astype(BF16)
        z = (jnp.dot(lh, rh, preferred_element_type=F32)
             + jnp.dot(lh, rl, preferred_element_type=F32)
             + jnp.dot(ll, rh, preferred_element_type=F32))
        row_z = lax.broadcasted_iota(jnp.int32, (LANES, tw), 0)
        lane_z = lax.broadcasted_iota(jnp.int32, (LANES, tw), 1)
        z = jnp.where(row_z // hh == (lane_z % LANES) // hh, z, 0.0)
        d_diag = jnp.where(lax.broadcasted_iota(jnp.int32, (LANES, LANES), 0)
                           == lax.broadcasted_iota(jnp.int32, (LANES, LANES), 1), d_ref[0], 0.0)
        z = jnp.concatenate([z[:, 0:LANES] + d_diag, z[:, LANES:]], axis=1)
        zb = z.astype(BF16)
        for s_in in range(t):
            if s_in:
                m_ref[s_in * LANES:(s_in + 1) * LANES, 0:s_in * LANES] = jnp.zeros((LANES, s_in * LANES), BF16)
            m_ref[s_in * LANES:(s_in + 1) * LANES, s_in * LANES:tw] = zb[:, 0:tw - s_in * LANES]

    u = u_ref[...].astype(BF16)
    y = jnp.concatenate(
        [jnp.dot(u[:, 0:(i + 2) * LANES], m_ref[0:(i + 2) * LANES, i * LANES:(i + 2) * LANES],
                 preferred_element_type=F32) for i in range(0, t, 2)], axis=1)
    x = lax.dot_general(u, qt_ref[...], (((1,), (1,)), ((), ())), preferred_element_type=F32)

    mr, mi, _, _ = _s5_discretize(lams_re_ref[0], lams_im_ref[0], jnp.exp(logdts_ref[0]))
    for _ in range(t.bit_length() - 1):
        mr, mi = _cmul(mr, mi, mr, mi)
    ridx = lax.broadcasted_iota(jnp.int32, (nc, sw), 0)

    def shift_down(v, k):
        if k % 8 == 0:
            return jnp.concatenate([jnp.zeros((k, sw), F32), v[:nc - k]], axis=0)
        return jnp.where(ridx >= k, pltpu.roll(v, k, axis=0), 0.0)

    s_re, s_im = x[:, 0:sw], x[:, sw:2 * sw]
    for k in range(n_cbits):
        h_re, h_im = shift_down(s_re, 1 << k), shift_down(s_im, 1 << k)
        s_re, s_im = s_re + (mr * h_re - mi * h_im), s_im + (mr * h_im + mi * h_re)
        mr, mi = _cmul(mr, mi, mr, mi)
    s_prev = jnp.concatenate([shift_down(s_re, 1), shift_down(s_im, 1)], axis=1).astype(BF16)
    y_ref[...] = y + jnp.dot(s_prev, pm_ref[...], preferred_element_type=F32)


def _s5_mix(u, lam_re, lam_im, log_dt, b_re, b_im, c_re, c_im, d_skip, n_batch):
    rows, _ = u.shape
    g, p, hh = b_re.shape
    t = S5_CHUNK
    gt = LANES // hh
    nt = g // gt
    tw = t * LANES
    nc = rows // n_batch

    def lanes_gh(a):
        return a.reshape(nt, gt, p, hh).transpose(0, 2, 1, 3).reshape(nt, p, LANES)

    def rows_gh(a):
        r = a.reshape(nt, gt, p, hh).transpose(0, 1, 3, 2).reshape(nt, LANES, p)
        return jnp.concatenate([r, r], axis=-1)

    per_gh = lambda a: jnp.broadcast_to(a.reshape(g, p, 1), (g, p, hh))
    lam_re_c, lam_im_c = lanes_gh(per_gh(lam_re)), lanes_gh(per_gh(lam_im))
    logdt_c = jnp.broadcast_to(log_dt.reshape(nt, gt, 1), (nt, gt, hh)).reshape(nt, 1, LANES)
    ct_re, ct_im = lanes_gh(c_re.transpose(0, 2, 1)), lanes_gh(c_im.transpose(0, 2, 1))
    b_re_c, b_im_c = lanes_gh(b_re), lanes_gh(b_im)
    lamr = lambda a: rows_gh(jnp.broadcast_to(a.reshape(g, p, 1), (g, p, hh)))
    logdt_r = jnp.broadcast_to(log_dt.reshape(nt, gt, 1), (nt, gt, hh)).reshape(nt, LANES, 1)
    d_c = d_skip.reshape(nt, 1, LANES)
    lams = lambda a: a.reshape(nt, 1, gt * p)
    logdt_s = jnp.broadcast_to(log_dt.reshape(nt, gt, 1), (nt, gt, p)).reshape(nt, 1, gt * p)

    params = [lam_re_c, lam_im_c, logdt_c, ct_re, ct_im, b_re_c, b_im_c,
              lamr(lam_re), lamr(lam_im), logdt_r, rows_gh(b_re), rows_gh(b_im), d_c,
              lams(lam_re), lams(lam_im), logdt_s]
    pspec = lambda a: pl.BlockSpec((1,) + a.shape[1:], lambda k, b: (k,) + (0,) * (a.ndim - 1))
    kern = functools.partial(_s5_kernel, chunk=t, n_chan=hh)
    return pl.pallas_call(
        kern,
        grid=(nt, n_batch),
        in_specs=[pl.BlockSpec((nc, tw), lambda k, b: (b, k))] + [pspec(a) for a in params],
        out_specs=pl.BlockSpec((nc, tw), lambda k, b: (b, k)),
        out_shape=jax.ShapeDtypeStruct(u.shape, F32),
        scratch_shapes=[pltpu.VMEM((tw, tw), BF16),
                        pltpu.VMEM((2 * gt * p, tw), BF16),
                        pltpu.VMEM((2 * gt * p, tw), BF16)],
        compiler_params=_cparams("arbitrary", "arbitrary"),
        name="s5_mix",
    )(u, *params)


HG_LOW_LEVELS = 3


def _hgrn_kernel(x_ref, lbl_ref, hn_ref, o_ref, st_ref, *, layer, n_heads, chunk):
    c = chunk
    n_sub = x_ref.shape[0] // c
    width = o_ref.shape[-1]
    dk = width // n_heads
    n_lev = c.bit_length() - 1

    @pl.when(pl.program_id(1) == 0)
    def _():
        st_ref[...] = jnp.zeros_like(st_ref)

    lg = lbl_ref[...]
    e = jnp.exp(lg - jnp.max(lg, axis=0, keepdims=True))
    sm = e / jnp.sum(e, axis=0, keepdims=True)
    lb = jnp.zeros((1, width), F32)
    for r in range(1, layer + 1):
        lb = lb + sm[r:r + 1]
    lb_floor = jnp.maximum(lb, 1e-30)
    one_m_lb = 1.0 - lb
    hnw = hn_ref[...]

    ri = lax.broadcasted_iota(jnp.int32, (c, c), 0)
    ci = lax.broadcasted_iota(jnp.int32, (c, c), 1)
    tri = jnp.where(ci <= ri, 1.0, 0.0).astype(BF16)
    keeps, lows = [], []
    for l in range(1, n_lev + 1):
        blk, half = 1 << l, 1 << (l - 1)
        keeps.append((ri // blk == ci // blk) & (ri % blk >= half) & (ci % blk < half))
        if l <= HG_LOW_LEVELS:
            bd = (ri // blk) * blk + half - 1
            lows.append(jnp.where(((ci > bd) & (ci <= ri)) | ((ci > ri) & (ci <= bd)), 1.0, 0.0))
    w_low = jnp.concatenate(lows, axis=0).astype(BF16)
    diag = ri == ci

    nt = (((1,), (1,)), ((), ()))
    tn = (((0,), (0,)), ((), ()))
    def decays(sub):
        rows = slice(sub * c, (sub + 1) * c)
        q = x_ref[rows, 0:width]
        zf = x_ref[rows, width:2 * width]
        sig = 1.0 / (1.0 + jnp.exp(-zf))
        logf = jnp.log(lb_floor + one_m_lb * sig)
        k = one_m_lb * (1.0 - sig)
        hi = logf.astype(BF16)
        mid = (logf - hi.astype(F32)).astype(BF16)
        b = jnp.dot(tri, hi, preferred_element_type=F32) + jnp.dot(tri, mid, preferred_element_type=F32)
        ex_low = (jnp.dot(w_low, hi, preferred_element_type=F32)
                  + jnp.dot(w_low, mid, preferred_element_type=F32))
        return q, k, b, ex_low

    def mix(sub, q, k, b, ex_low):
        rows = slice(sub * c, (sub + 1) * c)
        v = x_ref[rows, 2 * width:3 * width]
        gate = x_ref[rows, 3 * width:4 * width]
        qb, kb = q.astype(BF16), k.astype(BF16)
        att = [jnp.where(diag, jnp.sum((q * k)[:, h * dk:(h + 1) * dk], axis=-1, keepdims=True), 0.0)
               for h in range(n_heads)]
        for l in range(1, n_lev + 1):
            blk, half = 1 << l, 1 << (l - 1)
            if l <= HG_LOW_LEVELS:
                ex = ex_low[(l - 1) * c:l * c]
            else:
                pieces = []
                for s0 in range(0, c, blk):
                    bd = b[s0 + half - 1:s0 + half]
                    pieces.append(bd - b[s0:s0 + half])
                    pieces.append(b[s0 + half:s0 + blk] - bd)
                ex = jnp.concatenate(pieces, axis=0)
            wgt = jnp.exp(ex).astype(BF16)
            ql, kl = qb * wgt, kb * wgt
            for h in range(n_heads):
                sl = slice(h * dk, (h + 1) * dk)
                pm = lax.dot_general(ql[:, sl], kl[:, sl], nt, preferred_element_type=F32)
                att[h] = att[h] + jnp.where(keeps[l - 1], pm, 0.0)

        b_last = b[c - 1:c]
        q_in = (q * jnp.exp(b)).astype(BF16)
        k_out = (k * jnp.exp(b_last - b)).astype(BF16)
        e_last = jnp.exp(b_last)
        vb = v.astype(BF16)
        for h in range(n_heads):
            sl = slice(h * dk, (h + 1) * dk)
            st = st_ref[h]
            o = lax.dot_general(q_in[:, sl], st.astype(BF16), nt, preferred_element_type=F32)
            o = o + jnp.dot(att[h].astype(BF16), vb[:, sl], preferred_element_type=F32)
            st_ref[h] = st * e_last[:, sl] + lax.dot_general(vb[:, sl], k_out[:, sl], tn,
                                                              preferred_element_type=F32)
            o = o * lax.rsqrt(jnp.mean(o * o, axis=-1, keepdims=True) + EPS) * hnw[:, sl]
            gh = gate[:, sl]
            o_ref[rows, sl] = o * (gh * _sigmoid(gh))

    nxt = decays(0)
    for sub in range(n_sub):
        cur = nxt
        if sub + 1 < n_sub:
            nxt = decays(sub + 1)
        mix(sub, *cur)


def _hgrn_mix(x, lb_logits, head_norm, layer, n_batch):
    m = x.shape[0]
    n_layers, width = lb_logits.shape
    n_heads = head_norm.shape[0]
    dk = width // n_heads
    rows = HG_BLOCK
    per = m // n_batch // rows
    kern = functools.partial(_hgrn_kernel, layer=layer, n_heads=n_heads, chunk=HG_CHUNK)
    return pl.pallas_call(
        kern,
        grid=(n_batch, per),
        in_specs=[pl.BlockSpec((rows, 4 * width), lambda b, i: (b * per + i, 0)),
                  _const_spec((n_layers, width)),
                  _const_spec((1, width))],
        out_specs=pl.BlockSpec((rows, width), lambda b, i: (b * per + i, 0)),
        out_shape=jax.ShapeDtypeStruct((m, width), F32),
        scratch_shapes=[pltpu.VMEM((n_heads, dk, dk), F32)],
        compiler_params=_cparams("parallel", "arbitrary"),
        name="hgrn_mix",
    )(x, lb_logits, head_norm.reshape(1, width))


_STACK = (0, 2, 1, 3)


def _attn_kernel(sink_ref, q_ref, kv_ref, kvp_ref, o_ref, bias_ref, sinkc_ref, *, n_q, n_kv):
    tq = q_ref.shape[0]
    hd = q_ref.shape[1] // n_q
    grp = n_q // n_kv
    w = WINDOW
    lanes = 2 * hd
    nkw = n_kv * lanes
    first_block = pl.program_id(1) == 0

    @pl.when((pl.program_id(0) == 0) & first_block)
    def _():
        rr = lax.broadcasted_iota(jnp.int32, (2 * w, grp * w), 0)
        cc = lax.broadcasted_iota(jnp.int32, (2 * w, grp * w), 1)
        dist = cc % w + w - rr
        valid = (dist >= 0) & (dist < w)
        distf = dist.astype(F32)
        pos = cc // w
        posr = lax.broadcasted_iota(jnp.int32, (8, grp * w), 1) // w
        for kh in range(n_kv):
            slope = jnp.zeros((2 * w, grp * w), F32)
            sink = jnp.zeros((8, grp * w), F32)
            for i, g in enumerate(_STACK):
                head = kh * grp + g
                slope = jnp.where(pos == i, _LOG2E * 2.0 ** (-8.0 * (head + 1) / n_q), slope)
                sink = jnp.where(posr == i, _LOG2E * sink_ref[head], sink)
            b = jnp.where(valid, -slope * distf, NEG_BIG)
            bias_ref[0, kh] = b
            bias_ref[1, kh] = jnp.where(rr >= w, b, NEG_BIG)
            sinkc_ref[kh] = sink

    lane_q = lax.broadcasted_iota(jnp.int32, (w, lanes), 1)
    lo_q = lane_q < hd
    lane_v = lax.broadcasted_iota(jnp.int32, (2 * w, lanes), 1)
    lo_v = lane_v < hd
    zq = jnp.zeros((w, lanes), BF16)
    zv = jnp.zeros((2 * w, lanes), BF16)
    first_idx = jnp.where(first_block, 1, 0)

    nt = (((1,), (1,)), ((), ()))
    tn = (((0,), (0,)), ((), ()))

    def scores(jb, kh):
        rs = slice(jb * w, (jb + 1) * w)
        ks = slice(kh * lanes, (kh + 1) * lanes)
        if jb == 0:
            kprev = kvp_ref[:, ks]
            bias = bias_ref[first_idx, kh]
        else:
            kprev = kv_ref[(jb - 1) * w:jb * w, ks]
            bias = bias_ref[0, kh]
        kw = jnp.concatenate([kprev, kv_ref[rs, ks]], axis=0)
        qa = q_ref[rs, (2 * kh) * lanes:(2 * kh + 1) * lanes]
        qb = q_ref[rs, (2 * kh + 1) * lanes:(2 * kh + 2) * lanes]
        qs = jnp.concatenate([jnp.where(lo_q, qa, zq), jnp.where(lo_q, qb, zq),
                              jnp.where(lo_q, zq, qa), jnp.where(lo_q, zq, qb)], axis=0)
        return lax.dot_general(kw, qs, nt, preferred_element_type=F32) + bias

    def finish(jb, kh, st):
        rs = slice(jb * w, (jb + 1) * w)
        vs = slice(nkw + kh * lanes, nkw + (kh + 1) * lanes)
        vprev = kvp_ref[:, vs] if jb == 0 else kv_ref[(jb - 1) * w:jb * w, vs]
        vw = jnp.concatenate([vprev, kv_ref[rs, vs]], axis=0)
        sink = sinkc_ref[kh][0:1]
        mx = jnp.maximum(jnp.max(st, axis=0, keepdims=True), sink)
        pe = jnp.exp2(st - mx)
        den = jnp.sum(pe, axis=0, keepdims=True) + jnp.exp2(sink - mx)
        pn = (pe * (1.0 / den)).astype(BF16)
        o = (lax.dot_general(pn[:, 0:2 * w], jnp.where(lo_v, vw, zv), tn, preferred_element_type=F32)
             + lax.dot_general(pn[:, 2 * w:4 * w], jnp.where(lo_v, zv, vw), tn, preferred_element_type=F32))
        o_ref[rs, (2 * kh) * lanes:(2 * kh + 1) * lanes] = o[0:w].astype(o_ref.dtype)
        o_ref[rs, (2 * kh + 1) * lanes:(2 * kh + 2) * lanes] = o[w:2 * w].astype(o_ref.dtype)

    items = [(jb, kh) for jb in range(tq // w) for kh in range(n_kv)]
    pending = [scores(*items[i]) for i in range(ATTN_AHEAD)]
    for idx, item in enumerate(items):
        st = pending.pop(0)
        if idx + ATTN_AHEAD < len(items):
            pending.append(scores(*items[idx + ATTN_AHEAD]))
        finish(*item, st)


def _attention(q, kv, sinks, n_batch):
    m, qw = q.shape
    n_q = sinks.shape[0]
    hd = qw // n_q
    n_kv = kv.shape[1] // (4 * hd)
    grp = n_q // n_kv
    assert grp == 4 and 2 * hd == 128
    tq = ATTN_BLOCK_Q
    per = m // n_batch // tq
    ratio = tq // WINDOW
    kern = functools.partial(_attn_kernel, n_q=n_q, n_kv=n_kv)
    return pl.pallas_call(
        kern,
        grid=(n_batch, per),
        in_specs=[pl.BlockSpec(memory_space=pltpu.SMEM),
                  pl.BlockSpec((tq, qw), lambda b, i: (b * per + i, 0)),
                  pl.BlockSpec((tq, kv.shape[1]), lambda b, i: (b * per + i, 0)),
                  pl.BlockSpec((WINDOW, kv.shape[1]),
                               lambda b, i: (jnp.maximum((b * per + i) * ratio - 1, 0), 0))],
        out_specs=pl.BlockSpec((tq, qw), lambda b, i: (b * per + i, 0)),
        out_shape=jax.ShapeDtypeStruct((m, qw), BF16),
        scratch_shapes=[pltpu.VMEM((2, n_kv, 2 * WINDOW, grp * WINDOW), F32),
                        pltpu.VMEM((n_kv, 8, grp * WINDOW), F32)],
        compiler_params=_cparams("arbitrary", "arbitrary"),
        name="swa_attention",
    )(sinks, q, kv, kv)


def _dup_heads(wcols, n_heads):
    d = wcols.shape[0]
    wh = wcols.reshape(d, n_heads, -1)
    return jnp.concatenate([wh, wh], axis=-1).reshape(d, -1)


def _tail_kernel(*refs, even, final, ff_block):
    refs = list(refs)
    n_out = len(refs) - 1 if even else len(refs)
    o_ref = refs.pop(n_out - 1)
    h_ref = refs.pop(0)
    if even:
        ya_ref, yb_ref, wglu_ref, bglu_ref = refs[:4]
        refs = refs[4:]
    else:
        a_ref = refs.pop(0)
    p_ref, wout_ref, g2_ref, w1_ref, w2_ref, g3_ref, wup_ref, wgate_ref = refs[:8]
    gf_ref = refs[8] if final else None

    tm = h_ref.shape[0]
    halves = [slice(i * tm // TAIL_STREAMS, (i + 1) * tm // TAIL_STREAMS) for i in range(TAIL_STREAMS)]
    hs = [h_ref[r, :] for r in halves]
    if even:
        ya = _from_chunk_rows(ya_ref, refs[-1], S5_CHUNK)
        zs = [jax.nn.gelu(ya[r]) for r in halves]
        oas = [z * _sigmoid(_bdot(z, wglu_ref[...]) + bglu_ref[...]) for z in zs]
        wa = oas[0].shape[-1]
        mixes = [_bdot(oa, wout_ref[0:wa, :]) + _bdot(yb_ref[r, :], wout_ref[wa:, :]) for oa, r in zip(oas, halves)]
    else:
        mixes = [_bdot(a_ref[r, :], wout_ref[...]) for r in halves]
    hs = [h + mix for h, mix in zip(hs, mixes)]

    hns = [_rms(h, g2_ref[...]).astype(BF16) for h in hs]
    d_ff = w1_ref.shape[1]
    accs = [jnp.zeros_like(h) for h in hs]
    for c0 in range(0, d_ff, ff_block):
        for i, hn in enumerate(hns):
            a = jnp.dot(hn, w1_ref[:, c0:c0 + ff_block], preferred_element_type=F32)
            a = jnp.square(jnp.maximum(a, 0.0)).astype(BF16)
            accs[i] = accs[i] + jnp.dot(a, w2_ref[c0:c0 + ff_block, :], preferred_element_type=F32)
    hs = [h + acc for h, acc in zip(hs, accs)]

    hns = [_rms(h, g3_ref[...]) for h in hs]
    gates = [_sigmoid(_bdot(hn, wgate_ref[...])) for hn in hns]
    hs = [h + _bdot(p_ref[r, :], wup_ref[...]) * gate for h, gate, r in zip(hs, gates, halves)]
    for h, r in zip(hs, halves):
        o_ref[r, :] = _rms(h, gf_ref[...]) if final else h


def _tail(h, mixer_ins, mixer_weights, p, layer, w_out, g2, w1, w2, g3, w_up, w_gate, g_final, even):
    m, d = h.shape
    tm = TOKEN_BLOCK
    tok = lambda a: pl.BlockSpec((tm, a.shape[1]), lambda i: (i, 0))
    rowv = lambda a: a.reshape(1, -1)
    weights = [w_out, rowv(g2), w1, w2, rowv(g3), w_up, w_gate]
    stacked = (2, 3, 5, 6)
    final = g_final is not None
    if final:
        weights.append(rowv(g_final))
    args = [h, *mixer_ins, *mixer_weights, p, *weights]
    specs = ([tok(h)] + [tok(a) for a in mixer_ins] + [_const_spec(a.shape) for a in mixer_weights]
             + [pl.BlockSpec((None, tm, p.shape[-1]), lambda i: (layer, i, 0))]
             + [_layer_spec(a.shape, layer) if k in stacked else _const_spec(a.shape) for k, a in enumerate(weights)])
    scratch = []
    if even:
        ya = mixer_ins[0]
        specs[1] = pl.BlockSpec((tm // S5_CHUNK, ya.shape[1]), lambda i: (i, 0))
        scratch = [pltpu.VMEM((ya.shape[1] // S5_CHUNK // LANES, tm, LANES), F32)]
    kern = functools.partial(_tail_kernel, even=even, final=final, ff_block=1024)
    return pl.pallas_call(
        kern,
        grid=(m // tm,),
        in_specs=specs,
        out_specs=pl.BlockSpec((tm, d), lambda i: (i, 0)),
        out_shape=jax.ShapeDtypeStruct((m, d), F32),
        scratch_shapes=scratch,
        compiler_params=_cparams("parallel"),
        name="layer_tail",
    )(*args)


def kernel(x, p, mix_norm, mlp_norm, ple_norm, final_norm, w_in_even, w_out_even, s5_lam_re, s5_lam_im, s5_log_dt, s5_b_re, s5_b_im, s5_c_re, s5_c_im, s5_d, s5_w_glu, s5_b_glu, hgrn_lb_logits, hgrn_norm, w_qkv_odd, w_o_odd, attn_sinks, w_mlp_in, w_mlp_out, w_ple_up, w_ple_gate):
    bsz, seq, d = x.shape
    depth = p.shape[0]
    m = bsz * seq
    s5_w = s5_w_glu.shape[-1]
    hg_w = hgrn_lb_logits.shape[-1]
    n_q = attn_sinks.shape[-1]
    hd = d // n_q
    kv_w = w_qkv_odd.shape[-1] - n_q * hd
    assert seq % max(TOKEN_BLOCK, ATTN_BLOCK_Q, HG_BLOCK, S5_CHUNK) == 0 and m % PROJ_BLOCK == 0

    w1_all, w2_all = w_mlp_in.astype(BF16), w_mlp_out.astype(BF16)
    w_up_all, w_gate_all = w_ple_up.astype(BF16), w_ple_gate.astype(BF16)
    h = x.reshape(m, d)
    for i in range(depth):
        j = i // 2
        if i % 2 == 0:
            u, hg = _norm_proj(h, mix_norm[i], w_in_even[j].astype(BF16),
                               [(s5_w, F32, 1.0), (4 * hg_w, F32, 1.0)], chunk_rows=S5_CHUNK)
            ya = _s5_mix(u, s5_lam_re[j], s5_lam_im[j], s5_log_dt[j], s5_b_re[j], s5_b_im[j],
                         s5_c_re[j], s5_c_im[j], s5_d[j], bsz)
            yb = _hgrn_mix(hg, hgrn_lb_logits, hgrn_norm[j], j, bsz)
            mixer_ins = [ya, yb]
            mixer_weights = [s5_w_glu[j].astype(BF16), s5_b_glu[j].reshape(1, -1)]
            w_out = w_out_even[j]
        else:
            wq, wk, wv = jnp.split(w_qkv_odd[j], [n_q * hd, n_q * hd + kv_w // 2], axis=-1)
            n_kv = kv_w // (2 * hd)
            w_qkv = jnp.concatenate([wq, _dup_heads(wk, n_kv), _dup_heads(wv, n_kv)], axis=-1)
            q, kv = _norm_proj(h, mix_norm[i], w_qkv.astype(BF16),
                               [(n_q * hd, BF16, _LOG2E / math.sqrt(hd)), (2 * kv_w, BF16, 1.0)])
            mixer_ins = [_attention(q, kv, attn_sinks[j], bsz)]
            mixer_weights = []
            w_out = w_o_odd[j]
        h = _tail(h, mixer_ins, mixer_weights, p.reshape(depth, m, -1), i, w_out.astype(BF16),
                  mlp_norm[i], w1_all, w2_all, ple_norm[i], w_up_all, w_gate_all,
                  final_norm if i == depth - 1 else None, even=(i % 2 == 0))
    return h.reshape(bsz, seq, d)
```

```python
import functools
import math

import jax
import jax.numpy as jnp
from jax import lax
from jax.experimental import pallas as pl
from jax.experimental.pallas import tpu as pltpu

F32 = jnp.float32
BF16 = jnp.bfloat16

EPS = 1e-6
WINDOW = 128
S5_CHUNK = 16
LANES = 128
HG_CHUNK = 128
HG_BLOCK = 1024
LAM_RE_MAX = -1e-4
NEG_BIG = -1e30
_LOG2E = math.log2(math.e)

_V7X_VMEM_BYTES = 64 * 1024 * 1024
_VMEM_LIMIT = _V7X_VMEM_BYTES - 8 * 1024 * 1024

TOKEN_BLOCK = 512
TAIL_STREAMS = 2
PROJ_BLOCK = 1024
ATTN_BLOCK_Q = 2048
ATTN_AHEAD = 3


def _cparams(*sem):
    return pltpu.CompilerParams(dimension_semantics=sem, vmem_limit_bytes=_VMEM_LIMIT)


def _const_spec(shape):
    nd = len(shape)
    return pl.BlockSpec(shape, lambda *_: (0,) * nd, pipeline_mode=pl.Buffered(1))


def _layer_spec(shape, layer):
    nd = len(shape) - 1
    return pl.BlockSpec((None,) + tuple(shape[1:]), lambda *_: (layer,) + (0,) * nd,
                        pipeline_mode=pl.Buffered(1))


def _rms(x, g):
    ms = jnp.mean(x * x, axis=-1, keepdims=True)
    return x * lax.rsqrt(ms + EPS) * g


def _sigmoid(x):
    return 1.0 / (1.0 + jnp.exp(-x))


def _bdot(a, b):
    return jnp.dot(a.astype(BF16), b.astype(BF16), preferred_element_type=F32)


def _to_chunk_rows(val, o_ref, scr_ref, chunk):
    n_tiles = scr_ref.shape[0]
    rows = o_ref.shape[0]
    for k in range(n_tiles):
        scr_ref[k] = val[:, k * LANES:(k + 1) * LANES]
    for k in range(n_tiles):
        for t in range(chunk):
            o_ref[:, (k * chunk + t) * LANES:(k * chunk + t + 1) * LANES] = scr_ref[k, pl.ds(t, rows, stride=chunk), :]


def _from_chunk_rows(x_ref, scr_ref, chunk):
    n_tiles = scr_ref.shape[0]
    rows = x_ref.shape[0]
    for k in range(n_tiles):
        for t in range(chunk):
            scr_ref[k, pl.ds(t, rows, stride=chunk), :] = x_ref[:, (k * chunk + t) * LANES:(k * chunk + t + 1) * LANES]
    return jnp.concatenate([scr_ref[k] for k in range(n_tiles)], axis=1)


def _proj_kernel(h_ref, g_ref, w_ref, *refs, scales, widths, chunk_rows):
    out_refs = refs[:len(scales)]
    hn = _rms(h_ref[...], g_ref[...]).astype(BF16)
    off = 0
    for i, (o_ref, s, n) in enumerate(zip(out_refs, scales, widths)):
        acc = jnp.dot(hn, w_ref[:, off:off + n], preferred_element_type=F32)
        if s != 1.0:
            acc = acc * s
        if i == 0 and chunk_rows:
            _to_chunk_rows(acc, o_ref, refs[-1], chunk_rows)
        else:
            o_ref[...] = acc.astype(o_ref.dtype)
        off += n


def _norm_proj(h, g, w, outs, chunk_rows=0):
    m, d = h.shape
    tm = PROJ_BLOCK
    widths = tuple(n for n, _, _ in outs)
    kern = functools.partial(_proj_kernel, scales=tuple(s for _, _, s in outs), widths=widths,
                             chunk_rows=chunk_rows)
    out_specs = [pl.BlockSpec((tm, n), lambda i: (i, 0)) for n in widths]
    out_shape = [jax.ShapeDtypeStruct((m, n), dt) for n, dt, _ in outs]
    scratch = []
    if chunk_rows:
        n0 = widths[0]
        out_specs[0] = pl.BlockSpec((tm // chunk_rows, n0 * chunk_rows), lambda i: (i, 0))
        out_shape[0] = jax.ShapeDtypeStruct((m // chunk_rows, n0 * chunk_rows), outs[0][1])
        scratch = [pltpu.VMEM((n0 // LANES, tm, LANES), F32)]
    return pl.pallas_call(
        kern,
        grid=(m // tm,),
        in_specs=[pl.BlockSpec((tm, d), lambda i: (i, 0)),
                  _const_spec((1, d)),
                  _const_spec(w.shape)],
        out_specs=out_specs,
        out_shape=out_shape,
        scratch_shapes=scratch,
        compiler_params=_cparams("parallel"),
        name="norm_proj",
    )(h, g.reshape(1, d), w)


def _cmul(ar, ai, br, bi):
    return ar * br - ai * bi, ar * bi + ai * br


def _s5_discretize(lam_re, lam_im, dt):
    lr = jnp.minimum(lam_re, LAM_RE_MAX)
    li = lam_im
    mag = jnp.exp(lr * dt)
    ar = mag * jnp.cos(li * dt)
    ai = mag * jnp.sin(li * dt)
    den = lr * lr + li * li
    xr = ar - 1.0
    zr = (xr * lr + ai * li) / den
    zi = (ai * lr - xr * li) / den
    return ar, ai, zr, zi


def _s5_kernel(u_ref, lam_re_ref, lam_im_ref, logdt_ref, ct_re_ref, ct_im_ref, b_re_ref, b_im_ref,
               lamr_re_ref, lamr_im_ref, logdtr_ref, bt_re_ref, bt_im_ref, d_ref,
               lams_re_ref, lams_im_ref, logdts_ref,
               y_ref, m_ref, qt_ref, pm_ref, *, chunk, n_chan):
    t, hh = chunk, n_chan
    tw = t * LANES
    p = lam_re_ref.shape[1]
    gt = LANES // hh
    sw = gt * p
    nc = u_ref.shape[0]
    n_cbits = max(1, (nc - 1).bit_length())

    @pl.when(pl.program_id(1) == 0)
    def _():
        dt = jnp.exp(logdt_ref[0])
        ar, ai, zr, zi = _s5_discretize(lam_re_ref[0], lam_im_ref[0], dt)
        pw = [(jnp.ones((p, LANES), F32), jnp.zeros((p, LANES), F32))]
        for _ in range(t):
            pw.append(_cmul(pw[-1][0], pw[-1][1], ar, ai))
        bbr, bbi = _cmul(zr, zi, b_re_ref[0], b_im_ref[0])
        ctr, cti = ct_re_ref[0], ct_im_ref[0]
        lane_g = lax.broadcasted_iota(jnp.int32, (p, LANES), 1) // hh
        zero = jnp.zeros((p, LANES), F32)
        q_slots = [_cmul(pw[t - 1 - s][0], pw[t - 1 - s][1], bbr, bbi) for s in range(t)]
        c_slots = [_cmul(ctr, cti, pw[s + 1][0], pw[s + 1][1]) for s in range(t)]
        for g2 in range(gt):
            own = lane_g == g2
            rows_re = slice(g2 * p, (g2 + 1) * p)
            rows_im = slice(sw + g2 * p, sw + (g2 + 1) * p)
            qt_ref[rows_re, :] = jnp.concatenate([jnp.where(own, r, zero) for r, _ in q_slots], axis=1).astype(BF16)
            qt_ref[rows_im, :] = jnp.concatenate([jnp.where(own, i, zero) for _, i in q_slots], axis=1).astype(BF16)
            pm_ref[rows_re, :] = jnp.concatenate([jnp.where(own, r, zero) for r, _ in c_slots], axis=1).astype(BF16)
            pm_ref[rows_im, :] = jnp.concatenate([jnp.where(own, -i, zero) for _, i in c_slots], axis=1).astype(BF16)

        _, _, zr2, zi2 = _s5_discretize(lamr_re_ref[0], lamr_im_ref[0], jnp.exp(logdtr_ref[0]))
        first = lax.broadcasted_iota(jnp.int32, (LANES, 2 * p), 1) < p
        btr, bti = bt_re_ref[0], bt_im_ref[0]
        lhs = jnp.where(first, zr2 * btr - zi2 * bti, -(zr2 * bti + zi2 * btr))
        rhs = jnp.concatenate(
            [jnp.concatenate([_cmul(ctr, cti, pw[tau][0], pw[tau][1])[0] for tau in range(t)], axis=1),
             jnp.concatenate([_cmul(ctr, cti, pw[tau][0], pw[tau][1])[1] for tau in range(t)], axis=1)],
            axis=0)
        lh, ll = lhs.astype(BF16), (lhs - lhs.astype(BF16).astype(F32)).astype(BF16)
        rh, rl = rhs.astype(BF16), (rhs - rhs.astype(BF16).astype(F32)).astype(BF16)
        z = (jnp.dot(lh, rh, preferred_element_type=F32)
             + jnp.dot(lh, rl, preferred_element_type=F32)
             + jnp.dot(ll, rh, preferred_element_type=F32))
        row_z = lax.broadcasted_iota(jnp.int32, (LANES, tw), 0)
        lane_z = lax.broadcasted_iota(jnp.int32, (LANES, tw), 1)
        z = jnp.where(row_z // hh == (lane_z % LANES) // hh, z, 0.0)
        d_diag = jnp.where(lax.broadcasted_iota(jnp.int32, (LANES, LANES), 0)
                           == lax.broadcasted_iota(jnp.int32, (LANES, LANES), 1), d_ref[0], 0.0)
        z = jnp.concatenate([z[:, 0:LANES] + d_diag, z[:, LANES:]], axis=1)
        zb = z.astype(BF16)
        for s_in in range(t):
            if s_in:
                m_ref[s_in * LANES:(s_in + 1) * LANES, 0:s_in * LANES] = jnp.zeros((LANES, s_in * LANES), BF16)
            m_ref[s_in * LANES:(s_in + 1) * LANES, s_in * LANES:tw] = zb[:, 0:tw - s_in * LANES]

    u = u_ref[...].astype(BF16)
    y = jnp.concatenate(
        [jnp.dot(u[:, 0:(i + 2) * LANES], m_ref[0:(i + 2) * LANES, i * LANES:(i + 2) * LANES],
                 preferred_element_type=F32) for i in range(0, t, 2)], axis=1)
    x = lax.dot_general(u, qt_ref[...], (((1,), (1,)), ((), ())), preferred_element_type=F32)

    mr, mi, _, _ = _s5_discretize(lams_re_ref[0], lams_im_ref[0], jnp.exp(logdts_ref[0]))
    for _ in range(t.bit_length() - 1):
        mr, mi = _cmul(mr, mi, mr, mi)
    ridx = lax.broadcasted_iota(jnp.int32, (nc, sw), 0)

    def shift_down(v, k):
        if k % 8 == 0:
            return jnp.concatenate([jnp.zeros((k, sw), F32), v[:nc - k]], axis=0)
        return jnp.where(ridx >= k, pltpu.roll(v, k, axis=0), 0.0)

    s_re, s_im = x[:, 0:sw], x[:, sw:2 * sw]
    for k in range(n_cbits):
        h_re, h_im = shift_down(s_re, 1 << k), shift_down(s_im, 1 << k)
        s_re, s_im = s_re + (mr * h_re - mi * h_im), s_im + (mr * h_im + mi * h_re)
        mr, mi = _cmul(mr, mi, mr, mi)
    s_prev = jnp.concatenate([shift_down(s_re, 1), shift_down(s_im, 1)], axis=1).astype(BF16)
    y_ref[...] = y + jnp.dot(s_prev, pm_ref[...], preferred_element_type=F32)


def _s5_mix(u, lam_re, lam_im, log_dt, b_re, b_im, c_re, c_im, d_skip, n_batch):
    rows, _ = u.shape
    g, p, hh = b_re.shape
    t = S5_CHUNK
    gt = LANES // hh
    nt = g // gt
    tw = t * LANES
    nc = rows // n_batch

    def lanes_gh(a):
        return a.reshape(nt, gt, p, hh).transpose(0, 2, 1, 3).reshape(nt, p, LANES)

    def rows_gh(a):
        r = a.reshape(nt, gt, p, hh).transpose(0, 1, 3, 2).reshape(nt, LANES, p)
        return jnp.concatenate([r, r], axis=-1)

    per_gh = lambda a: jnp.broadcast_to(a.reshape(g, p, 1), (g, p, hh))
    lam_re_c, lam_im_c = lanes_gh(per_gh(lam_re)), lanes_gh(per_gh(lam_im))
    logdt_c = jnp.broadcast_to(log_dt.reshape(nt, gt, 1), (nt, gt, hh)).reshape(nt, 1, LANES)
    ct_re, ct_im = lanes_gh(c_re.transpose(0, 2, 1)), lanes_gh(c_im.transpose(0, 2, 1))
    b_re_c, b_im_c = lanes_gh(b_re), lanes_gh(b_im)
    lamr = lambda a: rows_gh(jnp.broadcast_to(a.reshape(g, p, 1), (g, p, hh)))
    logdt_r = jnp.broadcast_to(log_dt.reshape(nt, gt, 1), (nt, gt, hh)).reshape(nt, LANES, 1)
    d_c = d_skip.reshape(nt, 1, LANES)
    lams = lambda a: a.reshape(nt, 1, gt * p)
    logdt_s = jnp.broadcast_to(log_dt.reshape(nt, gt, 1), (nt, gt, p)).reshape(nt, 1, gt * p)

    params = [lam_re_c, lam_im_c, logdt_c, ct_re, ct_im, b_re_c, b_im_c,
              lamr(lam_re), lamr(lam_im), logdt_r, rows_gh(b_re), rows_gh(b_im), d_c,
              lams(lam_re), lams(lam_im), logdt_s]
    pspec = lambda a: pl.BlockSpec((1,) + a.shape[1:], lambda k, b: (k,) + (0,) * (a.ndim - 1))
    kern = functools.partial(_s5_kernel, chunk=t, n_chan=hh)
    return pl.pallas_call(
        kern,
        grid=(nt, n_batch),
        in_specs=[pl.BlockSpec((nc, tw), lambda k, b: (b, k))] + [pspec(a) for a in params],
        out_specs=pl.BlockSpec((nc, tw), lambda k, b: (b, k)),
        out_shape=jax.ShapeDtypeStruct(u.shape, F32),
        scratch_shapes=[pltpu.VMEM((tw, tw), BF16),
                        pltpu.VMEM((2 * gt * p, tw), BF16),
                        pltpu.VMEM((2 * gt * p, tw), BF16)],
        compiler_params=_cparams("arbitrary", "arbitrary"),
        name="s5_mix",
    )(u, *params)


HG_LOW_LEVELS = 3


def _hgrn_kernel(x_ref, lbl_ref, hn_ref, o_ref, st_ref, *, layer, n_heads, chunk):
    c = chunk
    n_sub = x_ref.shape[0] // c
    width = o_ref.shape[-1]
    dk = width // n_heads
    n_lev = c.bit_length() - 1

    @pl.when(pl.program_id(1) == 0)
    def _():
        st_ref[...] = jnp.zeros_like(st_ref)

    lg = lbl_ref[...]
    e = jnp.exp(lg - jnp.max(lg, axis=0, keepdims=True))
    sm = e / jnp.sum(e, axis=0, keepdims=True)
    lb = jnp.zeros((1, width), F32)
    for r in range(1, layer + 1):
        lb = lb + sm[r:r + 1]
    lb_floor = jnp.maximum(lb, 1e-30)
    one_m_lb = 1.0 - lb
    hnw = hn_ref[...]

    ri = lax.broadcasted_iota(jnp.int32, (c, c), 0)
    ci = lax.broadcasted_iota(jnp.int32, (c, c), 1)
    tri = jnp.where(ci <= ri, 1.0, 0.0).astype(BF16)
    keeps, lows = [], []
    for l in range(1, n_lev + 1):
        blk, half = 1 << l, 1 << (l - 1)
        keeps.append((ri // blk == ci // blk) & (ri % blk >= half) & (ci % blk < half))
        if l <= HG_LOW_LEVELS:
            bd = (ri // blk) * blk + half - 1
            lows.append(jnp.where(((ci > bd) & (ci <= ri)) | ((ci > ri) & (ci <= bd)), 1.0, 0.0))
    w_low = jnp.concatenate(lows, axis=0).astype(BF16)
    diag = ri == ci

    nt = (((1,), (1,)), ((), ()))
    tn = (((0,), (0,)), ((), ()))
    def decays(sub):
        rows = slice(sub * c, (sub + 1) * c)
        q = x_ref[rows, 0:width]
        zf = x_ref[rows, width:2 * width]
        sig = 1.0 / (1.0 + jnp.exp(-zf))
        logf = jnp.log(lb_floor + one_m_lb * sig)
        k = one_m_lb * (1.0 - sig)
        hi = logf.astype(BF16)
        mid = (logf - hi.astype(F32)).astype(BF16)
        b = jnp.dot(tri, hi, preferred_element_type=F32) + jnp.dot(tri, mid, preferred_element_type=F32)
        ex_low = (jnp.dot(w_low, hi, preferred_element_type=F32)
                  + jnp.dot(w_low, mid, preferred_element_type=F32))
        return q, k, b, ex_low

    def mix(sub, q, k, b, ex_low):
        rows = slice(sub * c, (sub + 1) * c)
        v = x_ref[rows, 2 * width:3 * width]
        gate = x_ref[rows, 3 * width:4 * width]
        qb, kb = q.astype(BF16), k.astype(BF16)
        att = [jnp.where(diag, jnp.sum((q * k)[:, h * dk:(h + 1) * dk], axis=-1, keepdims=True), 0.0)
               for h in range(n_heads)]
        for l in range(1, n_lev + 1):
            blk, half = 1 << l, 1 << (l - 1)
            if l <= HG_LOW_LEVELS:
                ex = ex_low[(l - 1) * c:l * c]
            else:
                pieces = []
                for s0 in range(0, c, blk):
                    bd = b[s0 + half - 1:s0 + half]
                    pieces.append(bd - b[s0:s0 + half])
                    pieces.append(b[s0 + half:s0 + blk] - bd)
                ex = jnp.concatenate(pieces, axis=0)
            wgt = jnp.exp(ex).astype(BF16)
            ql, kl = qb * wgt, kb * wgt
            for h in range(n_heads):
                sl = slice(h * dk, (h + 1) * dk)
                pm = lax.dot_general(ql[:, sl], kl[:, sl], nt, preferred_element_type=F32)
                att[h] = att[h] + jnp.where(keeps[l - 1], pm, 0.0)

        b_last = b[c - 1:c]
        q_in = (q * jnp.exp(b)).astype(BF16)
        k_out = (k * jnp.exp(b_last - b)).astype(BF16)
        e_last = jnp.exp(b_last)
        vb = v.astype(BF16)
        for h in range(n_heads):
            sl = slice(h * dk, (h + 1) * dk)
            st = st_ref[h]
            o = lax.dot_general(q_in[:, sl], st.astype(BF16), nt, preferred_element_type=F32)
            o = o + jnp.dot(att[h].astype(BF16), vb[:, sl], preferred_element_type=F32)
            st_ref[h] = st * e_last[:, sl] + lax.dot_general(vb[:, sl], k_out[:, sl], tn,
                                                              preferred_element_type=F32)
            o = o * lax.rsqrt(jnp.mean(o * o, axis=-1, keepdims=True) + EPS) * hnw[:, sl]
            gh = gate[:, sl]
            o_ref[rows, sl] = o * (gh * _sigmoid(gh))

    nxt = decays(0)
    for sub in range(n_sub):
        cur = nxt
        if sub + 1 < n_sub:
            nxt = decays(sub + 1)
        mix(sub, *cur)


def _hgrn_mix(x, lb_logits, head_norm, layer, n_batch):
    m = x.shape[0]
    n_layers, width = lb_logits.shape
    n_heads = head_norm.shape[0]
    dk = width // n_heads
    rows = HG_BLOCK
    per = m // n_batch // rows
    kern = functools.partial(_hgrn_kernel, layer=layer, n_heads=n_heads, chunk=HG_CHUNK)
    return pl.pallas_call(
        kern,
        grid=(n_batch, per),
        in_specs=[pl.BlockSpec((rows, 4 * width), lambda b, i: (b * per + i, 0)),
                  _const_spec((n_layers, width)),
                  _const_spec((1, width))],
        out_specs=pl.BlockSpec((rows, width), lambda b, i: (b * per + i, 0)),
        out_shape=jax.ShapeDtypeStruct((m, width), F32),
        scratch_shapes=[pltpu.VMEM((n_heads, dk, dk), F32)],
        compiler_params=_cparams("parallel", "arbitrary"),
        name="hgrn_mix",
    )(x, lb_logits, head_norm.reshape(1, width))


_STACK = (0, 2, 1, 3)


def _attn_kernel(sink_ref, q_ref, kv_ref, kvp_ref, o_ref, bias_ref, sinkc_ref, *, n_q, n_kv):
    tq = q_ref.shape[0]
    hd = q_ref.shape[1] // n_q
    grp = n_q // n_kv
    w = WINDOW
    lanes = 2 * hd
    nkw = n_kv * lanes
    first_block = pl.program_id(1) == 0

    @pl.when((pl.program_id(0) == 0) & first_block)
    def _():
        rr = lax.broadcasted_iota(jnp.int32, (2 * w, grp * w), 0)
        cc = lax.broadcasted_iota(jnp.int32, (2 * w, grp * w), 1)
        dist = cc % w + w - rr
        valid = (dist >= 0) & (dist < w)
        distf = dist.astype(F32)
        pos = cc // w
        posr = lax.broadcasted_iota(jnp.int32, (8, grp * w), 1) // w
        for kh in range(n_kv):
            slope = jnp.zeros((2 * w, grp * w), F32)
            sink = jnp.zeros((8, grp * w), F32)
            for i, g in enumerate(_STACK):
                head = kh * grp + g
                slope = jnp.where(pos == i, _LOG2E * 2.0 ** (-8.0 * (head + 1) / n_q), slope)
                sink = jnp.where(posr == i, _LOG2E * sink_ref[head], sink)
            b = jnp.where(valid, -slope * distf, NEG_BIG)
            bias_ref[0, kh] = b
            bias_ref[1, kh] = jnp.where(rr >= w, b, NEG_BIG)
            sinkc_ref[kh] = sink

    lane_q = lax.broadcasted_iota(jnp.int32, (w, lanes), 1)
    lo_q = lane_q < hd
    lane_v = lax.broadcasted_iota(jnp.int32, (2 * w, lanes), 1)
    lo_v = lane_v < hd
    zq = jnp.zeros((w, lanes), BF16)
    zv = jnp.zeros((2 * w, lanes), BF16)
    first_idx = jnp.where(first_block, 1, 0)

    nt = (((1,), (1,)), ((), ()))
    tn = (((0,), (0,)), ((), ()))

    def scores(jb, kh):
        rs = slice(jb * w, (jb + 1) * w)
        ks = slice(kh * lanes, (kh + 1) * lanes)
        if jb == 0:
            kprev = kvp_ref[:, ks]
            bias = bias_ref[first_idx, kh]
        else:
            kprev = kv_ref[(jb - 1) * w:jb * w, ks]
            bias = bias_ref[0, kh]
        kw = jnp.concatenate([kprev, kv_ref[rs, ks]], axis=0)
        qa = q_ref[rs, (2 * kh) * lanes:(2 * kh + 1) * lanes]
        qb = q_ref[rs, (2 * kh + 1) * lanes:(2 * kh + 2) * lanes]
        qs = jnp.concatenate([jnp.where(lo_q, qa, zq), jnp.where(lo_q, qb, zq),
                              jnp.where(lo_q, zq, qa), jnp.where(lo_q, zq, qb)], axis=0)
        return lax.dot_general(kw, qs, nt, preferred_element_type=F32) + bias

    def finish(jb, kh, st):
        rs = slice(jb * w, (jb + 1) * w)
        vs = slice(nkw + kh * lanes, nkw + (kh + 1) * lanes)
        vprev = kvp_ref[:, vs] if jb == 0 else kv_ref[(jb - 1) * w:jb * w, vs]
        vw = jnp.concatenate([vprev, kv_ref[rs, vs]], axis=0)
        sink = sinkc_ref[kh][0:1]
        mx = jnp.maximum(jnp.max(st, axis=0, keepdims=True), sink)
        pe = jnp.exp2(st - mx)
        den = jnp.sum(pe, axis=0, keepdims=True) + jnp.exp2(sink - mx)
        pn = (pe * (1.0 / den)).astype(BF16)
        o = (lax.dot_general(pn[:, 0:2 * w], jnp.where(lo_v, vw, zv), tn, preferred_element_type=F32)
             + lax.dot_general(pn[:, 2 * w:4 * w], jnp.where(lo_v, zv, vw), tn, preferred_element_type=F32))
        o_ref[rs, (2 * kh) * lanes:(2 * kh + 1) * lanes] = o[0:w].astype(o_ref.dtype)
        o_ref[rs, (2 * kh + 1) * lanes:(2 * kh + 2) * lanes] = o[w:2 * w].astype(o_ref.dtype)

    items = [(jb, kh) for jb in range(tq // w) for kh in range(n_kv)]
    pending = [scores(*items[i]) for i in range(ATTN_AHEAD)]
    for idx, item in enumerate(items):
        st = pending.pop(0)
        if idx + ATTN_AHEAD < len(items):
            pending.append(scores(*items[idx + ATTN_AHEAD]))
        finish(*item, st)


def _attention(q, kv, sinks, n_batch):
    m, qw = q.shape
    n_q = sinks.shape[0]
    hd = qw // n_q
    n_kv = kv.shape[1] // (4 * hd)
    grp = n_q // n_kv
    assert grp == 4 and 2 * hd == 128
    tq = ATTN_BLOCK_Q
    per = m // n_batch // tq
    ratio = tq // WINDOW
    kern = functools.partial(_attn_kernel, n_q=n_q, n_kv=n_kv)
    return pl.pallas_call(
        kern,
        grid=(n_batch, per),
        in_specs=[pl.BlockSpec(memory_space=pltpu.SMEM),
                  pl.BlockSpec((tq, qw), lambda b, i: (b * per + i, 0)),
                  pl.BlockSpec((tq, kv.shape[1]), lambda b, i: (b * per + i, 0)),
                  pl.BlockSpec((WINDOW, kv.shape[1]),
                               lambda b, i: (jnp.maximum((b * per + i) * ratio - 1, 0), 0))],
        out_specs=pl.BlockSpec((tq, qw), lambda b, i: (b * per + i, 0)),
        out_shape=jax.ShapeDtypeStruct((m, qw), BF16),
        scratch_shapes=[pltpu.VMEM((2, n_kv, 2 * WINDOW, grp * WINDOW), F32),
                        pltpu.VMEM((n_kv, 8, grp * WINDOW), F32)],
        compiler_params=_cparams("arbitrary", "arbitrary"),
        name="swa_attention",
    )(sinks, q, kv, kv)


def _dup_heads(wcols, n_heads):
    d = wcols.shape[0]
    wh = wcols.reshape(d, n_heads, -1)
    return jnp.concatenate([wh, wh], axis=-1).reshape(d, -1)


def _tail_kernel(*refs, even, final, ff_block, next_scales):
    refs = list(refs)
    n_scr = 1 if even else 0
    n_o = 1 + len(next_scales)
    outs = refs[len(refs) - n_scr - n_o:len(refs) - n_scr]
    scr = refs[len(refs) - n_scr:]
    refs = refs[:len(refs) - n_scr - n_o]
    o_ref, next_refs = outs[0], outs[1:]
    if next_scales:
        gn_ref, wn_ref = refs[-2:]
    h_ref = refs.pop(0)
    if even:
        ya_ref, yb_ref, wglu_ref, bglu_ref = refs[:4]
        refs = refs[4:]
    else:
        a_ref = refs.pop(0)
    p_ref, wout_ref, g2_ref, w1_ref, w2_ref, g3_ref, wup_ref, wgate_ref = refs[:8]
    gf_ref = refs[8] if final else None

    tm = h_ref.shape[0]
    halves = [slice(i * tm // TAIL_STREAMS, (i + 1) * tm // TAIL_STREAMS) for i in range(TAIL_STREAMS)]
    hs = [h_ref[r, :] for r in halves]
    if even:
        ya = _from_chunk_rows(ya_ref, scr[0], S5_CHUNK)
        zs = [jax.nn.gelu(ya[r]) for r in halves]
        oas = [z * _sigmoid(_bdot(z, wglu_ref[...]) + bglu_ref[...]) for z in zs]
        wa = oas[0].shape[-1]
        mixes = [_bdot(oa, wout_ref[0:wa, :]) + _bdot(yb_ref[r, :], wout_ref[wa:, :]) for oa, r in zip(oas, halves)]
    else:
        mixes = [_bdot(a_ref[r, :], wout_ref[...]) for r in halves]
    hs = [h + mix for h, mix in zip(hs, mixes)]

    hns = [_rms(h, g2_ref[...]).astype(BF16) for h in hs]
    d_ff = w1_ref.shape[1]
    accs = [jnp.zeros_like(h) for h in hs]
    for c0 in range(0, d_ff, ff_block):
        for i, hn in enumerate(hns):
            a = jnp.dot(hn, w1_ref[:, c0:c0 + ff_block], preferred_element_type=F32)
            a = jnp.square(jnp.maximum(a, 0.0)).astype(BF16)
            accs[i] = accs[i] + jnp.dot(a, w2_ref[c0:c0 + ff_block, :], preferred_element_type=F32)
    hs = [h + acc for h, acc in zip(hs, accs)]

    hns = [_rms(h, g3_ref[...]) for h in hs]
    gates = [_sigmoid(_bdot(hn, wgate_ref[...])) for hn in hns]
    hs = [h + _bdot(p_ref[r, :], wup_ref[...]) * gate for h, gate, r in zip(hs, gates, halves)]
    for h, r in zip(hs, halves):
        o_ref[r, :] = _rms(h, gf_ref[...]) if final else h
    if next_scales:
        hns = [_rms(h, gn_ref[...]).astype(BF16) for h in hs]
        off = 0
        for n_ref, s in zip(next_refs, next_scales):
            n = n_ref.shape[-1]
            for hn, r in zip(hns, halves):
                acc = jnp.dot(hn, wn_ref[:, off:off + n], preferred_element_type=F32)
                n_ref[r, :] = (acc * s if s != 1.0 else acc).astype(n_ref.dtype)
            off += n


def _tail(h, mixer_ins, mixer_weights, p, layer, w_out, g2, w1, w2, g3, w_up, w_gate, g_final, even,
          next_proj=None):
    m, d = h.shape
    tm = TOKEN_BLOCK
    tok = lambda a: pl.BlockSpec((tm, a.shape[1]), lambda i: (i, 0))
    rowv = lambda a: a.reshape(1, -1)
    weights = [w_out, rowv(g2), w1, w2, rowv(g3), w_up, w_gate]
    stacked = (2, 3, 5, 6)
    final = g_final is not None
    if final:
        weights.append(rowv(g_final))
    args = [h, *mixer_ins, *mixer_weights, p, *weights]
    specs = ([tok(h)] + [tok(a) for a in mixer_ins] + [_const_spec(a.shape) for a in mixer_weights]
             + [pl.BlockSpec((None, tm, p.shape[-1]), lambda i: (layer, i, 0))]
             + [_layer_spec(a.shape, layer) if k in stacked else _const_spec(a.shape) for k, a in enumerate(weights)])
    scratch = []
    if even:
        ya = mixer_ins[0]
        specs[1] = pl.BlockSpec((tm // S5_CHUNK, ya.shape[1]), lambda i: (i, 0))
        scratch = [pltpu.VMEM((ya.shape[1] // S5_CHUNK // LANES, tm, LANES), F32)]
    out_specs = [pl.BlockSpec((tm, d), lambda i: (i, 0))]
    out_shape = [jax.ShapeDtypeStruct((m, d), F32)]
    next_scales = ()
    if next_proj is not None:
        g_next, w_next, next_outs = next_proj
        args += [rowv(g_next), w_next]
        specs += [_const_spec((1, d)), _const_spec(w_next.shape)]
        out_specs += [pl.BlockSpec((tm, n), lambda i: (i, 0)) for n, _, _ in next_outs]
        out_shape += [jax.ShapeDtypeStruct((m, n), dt) for n, dt, _ in next_outs]
        next_scales = tuple(s for _, _, s in next_outs)
    kern = functools.partial(_tail_kernel, even=even, final=final, ff_block=1024, next_scales=next_scales)
    return pl.pallas_call(
        kern,
        grid=(m // tm,),
        in_specs=specs,
        out_specs=out_specs,
        out_shape=out_shape,
        scratch_shapes=scratch,
        compiler_params=_cparams("parallel"),
        name="layer_tail",
    )(*args)


def kernel(x, p, mix_norm, mlp_norm, ple_norm, final_norm, w_in_even, w_out_even, s5_lam_re, s5_lam_im, s5_log_dt, s5_b_re, s5_b_im, s5_c_re, s5_c_im, s5_d, s5_w_glu, s5_b_glu, hgrn_lb_logits, hgrn_norm, w_qkv_odd, w_o_odd, attn_sinks, w_mlp_in, w_mlp_out, w_ple_up, w_ple_gate):
    bsz, seq, d = x.shape
    depth = p.shape[0]
    m = bsz * seq
    s5_w = s5_w_glu.shape[-1]
    hg_w = hgrn_lb_logits.shape[-1]
    n_q = attn_sinks.shape[-1]
    hd = d // n_q
    kv_w = w_qkv_odd.shape[-1] - n_q * hd
    assert seq % max(TOKEN_BLOCK, ATTN_BLOCK_Q, HG_BLOCK, S5_CHUNK) == 0 and m % PROJ_BLOCK == 0

    w1_all, w2_all = w_mlp_in.astype(BF16), w_mlp_out.astype(BF16)
    w_up_all, w_gate_all = w_ple_up.astype(BF16), w_ple_gate.astype(BF16)
    h = x.reshape(m, d)
    for i in range(depth):
        j = i // 2
        if i % 2 == 0:
            u, hg = _norm_proj(h, mix_norm[i], w_in_even[j].astype(BF16),
                               [(s5_w, F32, 1.0), (4 * hg_w, F32, 1.0)], chunk_rows=S5_CHUNK)
            ya = _s5_mix(u, s5_lam_re[j], s5_lam_im[j], s5_log_dt[j], s5_b_re[j], s5_b_im[j],
                         s5_c_re[j], s5_c_im[j], s5_d[j], bsz)
            yb = _hgrn_mix(hg, hgrn_lb_logits, hgrn_norm[j], j, bsz)
            mixer_ins = [ya, yb]
            mixer_weights = [s5_w_glu[j].astype(BF16), s5_b_glu[j].reshape(1, -1)]
            w_out = w_out_even[j]
            wq, wk, wv = jnp.split(w_qkv_odd[j], [n_q * hd, n_q * hd + kv_w // 2], axis=-1)
            n_kv = kv_w // (2 * hd)
            w_qkv = jnp.concatenate([wq, _dup_heads(wk, n_kv), _dup_heads(wv, n_kv)], axis=-1)
            next_proj = (mix_norm[i + 1], w_qkv.astype(BF16),
                         [(n_q * hd, BF16, _LOG2E / math.sqrt(hd)), (2 * kv_w, BF16, 1.0)])
        else:
            mixer_ins = [_attention(q, kv, attn_sinks[j], bsz)]
            mixer_weights = []
            w_out = w_o_odd[j]
            next_proj = None
        res = _tail(h, mixer_ins, mixer_weights, p.reshape(depth, m, -1), i, w_out.astype(BF16),
                    mlp_norm[i], w1_all, w2_all, ple_norm[i], w_up_all, w_gate_all,
                    final_norm if i == depth - 1 else None, even=(i % 2 == 0), next_proj=next_proj)
        h = res[0]
        if next_proj is not None:
            q, kv = res[1], res[2]
    return h.reshape(bsz, seq, d)
```
